```python
import math
import jax, jax.numpy as jnp
from jax import lax
import numpy as np

D_MODEL = 2048
BATCH = 2
SEQ = 4096
DEPTH = 1

RW_HEAD = 64
RW_HEADS = D_MODEL // RW_HEAD
RW_WIDTH = RW_HEADS * RW_HEAD
RW_DECAY_LORA = 96
RW_AAA_LORA = 96
RW_GATE_LORA = 256
RW_LNX_EPS = 64e-5
MB_EXPAND = 2
MB_WIDTH = MB_EXPAND * D_MODEL
MB_HEAD = 64
MB_HEADS = MB_WIDTH // MB_HEAD
MB_GROUPS = 8
MB_STATE = 128
MB_CONV = 4
MB_CHUNK = 128
MB_CONV_WIDTH = MB_WIDTH + 2 * MB_GROUPS * MB_STATE
D_FF = 4 * D_MODEL
NORM_EPS = 1e-5
OFF_RKV = 3 * RW_WIDTH
OFF_Z = OFF_RKV + MB_WIDTH
OFF_XBC = OFF_Z + MB_CONV_WIDTH
IN_WIDTH = OFF_XBC + MB_HEADS

kernel_name = 'rwkv7_mamba2_gated_hybrid_block'


def rms_norm(x, g, eps=NORM_EPS):
    xf = x.astype(jnp.float32)
    y = xf * lax.rsqrt(jnp.mean(xf * xf, axis=-1, keepdims=True) + eps)
    return (y * g.astype(jnp.float32)).astype(x.dtype)


def token_shift(u):
    return jnp.pad(u, ((0, 0), (1, 0), (0, 0)))[:, :-1]


def causal_depthwise_conv(u, w, b):
    K, ch = w.shape
    y = lax.conv_general_dilated(u, w[:, None, :].astype(u.dtype), window_strides=(1,),
                                 padding=[(K - 1, 0)], dimension_numbers=('NWC', 'WIO', 'NWC'),
                                 feature_group_count=ch)
    return y + b


def rwkv7_recurrence(r, w, k, v, a, b):
    Bsz, T, H, N = r.shape

    def step(S, inp):
        r_t, w_t, k_t, v_t, a_t, b_t = inp
        sa = jnp.einsum('bhij,bhj->bhi', S, a_t)
        S = S * w_t[:, :, None, :] + sa[..., None] * b_t[:, :, None, :] + v_t[..., None] * k_t[:, :, None, :]
        return S, jnp.einsum('bhij,bhj->bhi', S, r_t)

    xs = (jnp.moveaxis(r, 1, 0), jnp.moveaxis(w, 1, 0), jnp.moveaxis(k, 1, 0),
          jnp.moveaxis(v, 1, 0), jnp.moveaxis(a, 1, 0), jnp.moveaxis(b, 1, 0))
    S0 = jnp.zeros((Bsz, H, N, N), jnp.float32)
    _, ys = lax.scan(step, S0, xs)
    return jnp.moveaxis(ys, 0, 1)


def rwkv7_branch(h, p_rkv, mu_rkv, mu_wag, w0, w1, w2, a0, a1, a2, g1, g2, k_k, k_a, r_k, lnx_w, lnx_b):
    f32 = jnp.float32
    Bsz, T, _ = h.shape
    H, N = RW_HEADS, RW_HEAD
    p_rkv = p_rkv + (token_shift(p_rkv) - p_rkv) * mu_rkv
    r, k, v = jnp.split(p_rkv, 3, axis=-1)
    dh = token_shift(h) - h
    xw = h + dh * mu_wag[0]
    xa = h + dh * mu_wag[1]
    xg = h + dh * mu_wag[2]
    w_log = -jax.nn.softplus(-(w0 + jnp.tanh(xw @ w1) @ w2).astype(f32)) - 0.5
    decay = jnp.exp(-jnp.exp(w_log))
    a = jax.nn.sigmoid((a0 + (xa @ a1) @ a2).astype(f32))
    g = (jax.nn.sigmoid(xg @ g1) @ g2).astype(f32)
    heads = lambda t: t.astype(f32).reshape(Bsz, T, H, N)
    r, k, v, decay, a = heads(r), heads(k), heads(v), heads(decay), heads(a)
    kk = k * k_k.astype(f32).reshape(H, N)
    kk = kk / jnp.maximum(jnp.sqrt(jnp.sum(kk * kk, axis=-1, keepdims=True)), 1e-12)
    k = k * (1.0 + (a - 1.0) * k_a.astype(f32).reshape(H, N))
    y = rwkv7_recurrence(r, decay, k, v, -kk, kk * a)
    mu = jnp.mean(y, axis=-1, keepdims=True)
    var = jnp.mean(jnp.square(y - mu), axis=-1, keepdims=True)
    y = (y - mu) * lax.rsqrt(var + RW_LNX_EPS) * lnx_w.astype(f32).reshape(H, N) + lnx_b.astype(f32).reshape(H, N)
    y = y + jnp.sum(r * k * r_k.astype(f32), axis=-1, keepdims=True) * v
    return (y.reshape(Bsz, T, RW_WIDTH) * g).astype(h.dtype)


def ssd_chunked(X, dA, Bm, Cm):
    Bsz, T, H, P = X.shape
    G, N = Bm.shape[2], Bm.shape[3]
    J = H // G
    CH = MB_CHUNK
    NC = T // CH
    X = X.reshape(Bsz, NC, CH, G, J, P)
    Bm = Bm.reshape(Bsz, NC, CH, G, N)
    Cm = Cm.reshape(Bsz, NC, CH, G, N)
    dA = dA.reshape(Bsz, NC, CH, G, J).transpose(0, 1, 3, 4, 2)
    Acs = jnp.cumsum(dA, axis=-1)
    seg = Acs[..., :, None] - Acs[..., None, :]
    causal = jnp.tril(jnp.ones((CH, CH), dtype=bool))
    Ldec = jnp.exp(jnp.where(causal, seg, -jnp.inf))
    CB = jnp.einsum('bclgn,bcsgn->bcgls', Cm, Bm)
    scores = CB[:, :, :, None] * Ldec
    y_diag = jnp.einsum('bcgjls,bcsgjp->bclgjp', scores, X)
    decay_states = jnp.exp(Acs[..., -1:] - Acs)
    states = jnp.einsum('bclgn,bcgjl,bclgjp->bcgjpn', Bm, decay_states, X)
    chunk_decay = jnp.exp(Acs[..., -1])

    def step(S, inp):
        dec, st = inp
        return S * dec[..., None, None] + st, S

    S0 = jnp.zeros((Bsz, G, J, P, N), jnp.float32)
    _, S_in = lax.scan(step, S0, (jnp.moveaxis(chunk_decay, 1, 0), jnp.moveaxis(states, 1, 0)))
    S_in = jnp.moveaxis(S_in, 0, 1)
    y_off = jnp.einsum('bclgn,bcgjpn,bcgjl->bclgjp', Cm, S_in, jnp.exp(Acs))
    return (y_diag + y_off).reshape(Bsz, T, H, P)


def mamba2_branch(z, xbc, dt_raw, conv_w, conv_b, dt_bias, a_log, d_skip, norm_w):
    f32 = jnp.float32
    Bsz, T, _ = z.shape
    xbc = jax.nn.silu(causal_depthwise_conv(xbc, conv_w, conv_b))
    xs, bm, cm = jnp.split(xbc, [MB_WIDTH, MB_WIDTH + MB_GROUPS * MB_STATE], axis=-1)
    xs = xs.astype(f32).reshape(Bsz, T, MB_HEADS, MB_HEAD)
    bm = bm.astype(f32).reshape(Bsz, T, MB_GROUPS, MB_STATE)
    cm = cm.astype(f32).reshape(Bsz, T, MB_GROUPS, MB_STATE)
    dt = jax.nn.softplus(dt_raw.astype(f32) + dt_bias.astype(f32))
    A = -jnp.exp(a_log.astype(f32))
    y = ssd_chunked(xs * dt[..., None], dt * A, bm, cm) + xs * d_skip.astype(f32)[:, None]
    yg = (y.reshape(Bsz, T, MB_WIDTH) * jax.nn.silu(z.astype(f32))).reshape(Bsz, T, MB_GROUPS, -1)
    yg = yg * lax.rsqrt(jnp.mean(yg * yg, axis=-1, keepdims=True) + NORM_EPS)
    return (yg.reshape(Bsz, T, MB_WIDTH) * norm_w.astype(f32)).astype(z.dtype)


def setup_inputs(seed: int = 0) -> dict:
    key = jax.random.key(seed)
    ks = iter(jax.random.split(key, 48))
    f32 = jnp.float32
    L, D, W = DEPTH, D_MODEL, RW_WIDTH

    def nrm(shape, scale):
        return jax.random.normal(next(ks), shape, f32) * scale

    def unif(shape, lo, hi):
        return jax.random.uniform(next(ks), shape, f32, lo, hi)

    dt0 = jnp.exp(unif((L, MB_HEADS), math.log(1e-3), math.log(1e-1)))
    return {
        'x': nrm((BATCH, SEQ, D), 1.0),
        'c': nrm((BATCH, D), 1.0),
        'ada_w': nrm((L, D, 6 * D), 0.5 * D ** -0.5),
        'ada_b': nrm((L, 6 * D), 0.02),
        'norm1_g': 1.0 + nrm((L, D), 0.02),
        'w_in': nrm((L, D, IN_WIDTH), D ** -0.5),
        'rw_mu_rkv': unif((L, 3 * W), 0.0, 1.0),
        'rw_mu_wag': unif((L, 3, D), 0.0, 1.0),
        'rw_w0': unif((L, W), -6.0, -1.0),
        'rw_w1': nrm((L, D, RW_DECAY_LORA), D ** -0.5),
        'rw_w2': nrm((L, RW_DECAY_LORA, W), 0.1 * RW_DECAY_LORA ** -0.5),
        'rw_a0': nrm((L, W), 0.1),
        'rw_a1': nrm((L, D, RW_AAA_LORA), D ** -0.5),
        'rw_a2': nrm((L, RW_AAA_LORA, W), 0.5 * RW_AAA_LORA ** -0.5),
        'rw_g1': nrm((L, D, RW_GATE_LORA), D ** -0.5),
        'rw_g2': nrm((L, RW_GATE_LORA, W), RW_GATE_LORA ** -0.5),
        'rw_k_k': 0.85 + nrm((L, W), 0.02),
        'rw_k_a': 1.0 + nrm((L, W), 0.02),
        'rw_r_k': nrm((L, RW_HEADS, RW_HEAD), 0.1),
        'rw_lnx_w': 1.0 + nrm((L, W), 0.02),
        'rw_lnx_b': nrm((L, W), 0.02),
        'mb_conv_w': nrm((L, MB_CONV, MB_CONV_WIDTH), MB_CONV ** -0.5),
        'mb_conv_b': nrm((L, MB_CONV_WIDTH), 0.02),
        'mb_dt_bias': dt0 + jnp.log(-jnp.expm1(-dt0)),
        'mb_a_log': jnp.log(unif((L, MB_HEADS), 1.0, 16.0)),
        'mb_d': 1.0 + nrm((L, MB_HEADS), 0.02),
        'mb_norm_w': 1.0 + nrm((L, MB_WIDTH), 0.02),
        'w_branch_a': nrm((L, W, D), W ** -0.5),
        'w_branch_b': nrm((L, MB_WIDTH, D), MB_WIDTH ** -0.5),
        'w_gate': nrm((L, D, 2 * D), D ** -0.5),
        'b_gate': nrm((L, 2 * D), 0.02),
        'w_out': nrm((L, D, D), D ** -0.5),
        'norm2_g': 1.0 + nrm((L, D), 0.02),
        'mlp_up': nrm((L, D, D_FF), D ** -0.5),
        'mlp_down': nrm((L, D_FF, D), D_FF ** -0.5),
        'final_norm_g': 1.0 + nrm((D,), 0.02),
    }


def reference(x, c, ada_w, ada_b, norm1_g, w_in, rw_mu_rkv, rw_mu_wag, rw_w0, rw_w1, rw_w2,
              rw_a0, rw_a1, rw_a2, rw_g1, rw_g2, rw_k_k, rw_k_a, rw_r_k, rw_lnx_w, rw_lnx_b,
              mb_conv_w, mb_conv_b, mb_dt_bias, mb_a_log, mb_d, mb_norm_w,
              w_branch_a, w_branch_b, w_gate, b_gate, w_out, norm2_g, mlp_up, mlp_down,
              final_norm_g):
    c_act = jax.nn.silu(c)
    for l in range(DEPTH):
        mod = c_act @ ada_w[l] + ada_b[l]
        sh1, sc1, gt1, sh2, sc2, gt2 = [m[:, None, :] for m in jnp.split(mod, 6, axis=-1)]
        h = rms_norm(x, norm1_g[l]) * (1.0 + sc1) + sh1
        proj = h @ w_in[l]
        o_a = rwkv7_branch(h, proj[..., :OFF_RKV], rw_mu_rkv[l], rw_mu_wag[l], rw_w0[l], rw_w1[l], rw_w2[l],
                           rw_a0[l], rw_a1[l], rw_a2[l], rw_g1[l], rw_g2[l], rw_k_k[l], rw_k_a[l],
                           rw_r_k[l], rw_lnx_w[l], rw_lnx_b[l])
        o_b = mamba2_branch(proj[..., OFF_RKV:OFF_Z], proj[..., OFF_Z:OFF_XBC], proj[..., OFF_XBC:],
                            mb_conv_w[l], mb_conv_b[l], mb_dt_bias[l], mb_a_log[l], mb_d[l],
                            mb_norm_w[l])
        g_a, g_b = jnp.split(jax.nn.sigmoid(h @ w_gate[l] + b_gate[l]), 2, axis=-1)
        merged = g_a * (o_a @ w_branch_a[l]) + g_b * (o_b @ w_branch_b[l])
        x = x + gt1 * (merged @ w_out[l])
        h2 = rms_norm(x, norm2_g[l]) * (1.0 + sc2) + sh2
        u = jax.nn.relu(h2 @ mlp_up[l])
        x = x + gt2 * ((u * u) @ mlp_down[l])
    return rms_norm(x, final_norm_g)
```

```python
import functools

import jax
import jax.numpy as jnp
from jax import lax
from jax.experimental import pallas as pl
from jax.experimental.pallas import tpu as pltpu

F32 = jnp.float32
BF16 = jnp.bfloat16

LANES = 128
HEAD = 64
RW_CHUNK = 64
RW_DECAY_LORA = 96
RW_AAA_LORA = 96
RW_GATE_LORA = 256
RW_LNX_EPS = 64e-5
MB_GROUPS = 8
MB_STATE = 128
MB_CONV = 4
MB_CHUNK = 128
NORM_EPS = 1e-5
VMEM_LIMIT = 56 * 1024 * 1024

SM_W, SM_A, SM_G, SM_DT, SM_WIDTH = 0, 128, 256, 512, 640


def _params(sem):
    return pltpu.CompilerParams(dimension_semantics=sem, vmem_limit_bytes=VMEM_LIMIT)


def _sigmoid(x):
    return 1.0 / (1.0 + jnp.exp(-x))


def _softplus(x):
    return jnp.maximum(x, 0.0) + jnp.log1p(jnp.exp(-jnp.abs(x)))


def _dot(a, b):
    return jnp.dot(a, b, preferred_element_type=F32)


def _dot_nt(a, b):
    return lax.dot_general(a, b, (((1,), (1,)), ((), ())), preferred_element_type=F32)


def _dot_tn(a, b):
    return lax.dot_general(a, b, (((0,), (0,)), ((), ())), preferred_element_type=F32)


def _split3(x):
    hi = x.astype(BF16)
    r1 = x - hi.astype(F32)
    mid = r1.astype(BF16)
    lo = (r1 - mid.astype(F32)).astype(BF16)
    return hi, mid, lo


def _dot3(dot_fn, x, m, x_is_lhs):
    parts = _split3(x)
    if x_is_lhs:
        return dot_fn(parts[0], m) + dot_fn(parts[1], m) + dot_fn(parts[2], m)
    return dot_fn(m, parts[0]) + dot_fn(m, parts[1]) + dot_fn(m, parts[2])


def _ada_kernel(c_ref, w_ref, b_ref, o_ref):
    c = c_ref[...]
    ca = (c * _sigmoid(c)).astype(BF16)
    o_ref[...] = _dot(ca, w_ref[...].astype(BF16)) + b_ref[...]


def _ada_mod(c, ada_w, ada_b):
    bsz, d = c.shape
    n = ada_w.shape[1]
    rows = 8
    c_pad = jnp.zeros((rows, d), F32).at[:bsz].set(c)
    tn = 1024
    out = pl.pallas_call(
        _ada_kernel,
        grid=(n // tn,),
        in_specs=[pl.BlockSpec((rows, d), lambda j: (0, 0)),
                  pl.BlockSpec((d, tn), lambda j: (0, j)),
                  pl.BlockSpec((1, tn), lambda j: (0, j))],
        out_specs=pl.BlockSpec((rows, tn), lambda j: (0, j)),
        out_shape=jax.ShapeDtypeStruct((rows, n), F32),
        compiler_params=_params(("parallel",)),
        name="ada_mod",
    )(c_pad, ada_w, ada_b.reshape(1, n))
    return out[:bsz]


def _norm1_kernel(x_ref, g_ref, sc_ref, sh_ref, mu_ref, ws_ref, h_ref, sm_ref, prev_ref):
    i = pl.program_id(1)

    @pl.when(i == 0)
    def _():
        prev_ref[...] = jnp.zeros_like(prev_ref)

    x = x_ref[...]
    tm = x.shape[0]
    ms = jnp.mean(x * x, axis=-1, keepdims=True)
    h = x * lax.rsqrt(ms + NORM_EPS) * g_ref[...] * (1.0 + sc_ref[...]) + sh_ref[...]
    hs = pltpu.roll(h, 1, 0)
    row = lax.broadcasted_iota(jnp.int32, h.shape, 0)
    hs = jnp.where(row == 0, prev_ref[0:1, :], hs)
    prev_ref[0:1, :] = h[tm - 1:tm, :]
    dh = hs - h
    hb = h.astype(BF16)
    h_ref[...] = hb
    xw = (h + dh * mu_ref[0:1, :]).astype(BF16)
    sm_ref[:, SM_W:SM_A] = _dot(xw, ws_ref[:, SM_W:SM_A])
    xa = (h + dh * mu_ref[1:2, :]).astype(BF16)
    sm_ref[:, SM_A:SM_G] = _dot(xa, ws_ref[:, SM_A:SM_G])
    xg = (h + dh * mu_ref[2:3, :]).astype(BF16)
    sm_ref[:, SM_G:SM_DT] = _dot(xg, ws_ref[:, SM_G:SM_DT])
    sm_ref[:, SM_DT:SM_WIDTH] = _dot(hb, ws_ref[:, SM_DT:SM_WIDTH])


def _norm1(x, g, sc, sh, mu_wag, w_small, tm=256):
    bsz, t, d = x.shape
    return pl.pallas_call(
        _norm1_kernel,
        grid=(bsz, t // tm),
        in_specs=[pl.BlockSpec((None, tm, d), lambda b, i: (b, i, 0)),
                  pl.BlockSpec((1, d), lambda b, i: (0, 0)),
                  pl.BlockSpec((None, 1, d), lambda b, i: (b, 0, 0)),
                  pl.BlockSpec((None, 1, d), lambda b, i: (b, 0, 0)),
                  pl.BlockSpec((3, d), lambda b, i: (0, 0)),
                  pl.BlockSpec((d, SM_WIDTH), lambda b, i: (0, 0))],
        out_specs=[pl.BlockSpec((None, tm, d), lambda b, i: (b, i, 0)),
                   pl.BlockSpec((None, tm, SM_WIDTH), lambda b, i: (b, i, 0))],
        out_shape=[jax.ShapeDtypeStruct((bsz, t, d), BF16),
                   jax.ShapeDtypeStruct((bsz, t, SM_WIDTH), F32)],
        scratch_shapes=[pltpu.VMEM((8, d), F32)],
        compiler_params=_params(("parallel", "arbitrary")),
        name="norm1_small_proj",
    )(x, g.reshape(1, d), sc.reshape(bsz, 1, d), sh.reshape(bsz, 1, d), mu_wag, w_small)


def _mm_kernel(a_ref, b_ref, *rest, act, has_bias):
    if has_bias:
        bias_ref, o_ref = rest
    else:
        (o_ref,) = rest
    y = _dot(a_ref[...], b_ref[...])
    if has_bias:
        y = y + bias_ref[...]
    if act == "sigmoid":
        y = _sigmoid(y)
    elif act == "relu2":
        y = jnp.maximum(y, 0.0)
        y = y * y
    o_ref[...] = y.astype(o_ref.dtype)


def _matmul(a, b, out_dtype, bias=None, act=None, tm=1024, tn=1024, name="matmul"):
    m, k = a.shape
    n = b.shape[1]
    tm, tn = min(tm, m), min(tn, n)
    in_specs = [pl.BlockSpec((tm, k), lambda j, i: (i, 0)),
                pl.BlockSpec((k, tn), lambda j, i: (0, j))]
    args = [a, b]
    if bias is not None:
        in_specs.append(pl.BlockSpec((1, tn), lambda j, i: (0, j)))
        args.append(bias.reshape(1, n).astype(F32))
    return pl.pallas_call(
        functools.partial(_mm_kernel, act=act, has_bias=bias is not None),
        grid=(n // tn, m // tm),
        in_specs=in_specs,
        out_specs=pl.BlockSpec((tm, tn), lambda j, i: (i, j)),
        out_shape=jax.ShapeDtypeStruct((m, n), out_dtype),
        compiler_params=_params(("parallel", "parallel")),
        name=name,
    )(*args)


def _merge_kernel(oa_ref, ob_ref, wa_ref, wb_ref, ga_ref, gb_ref, o_ref):
    ya = _dot(oa_ref[...], wa_ref[...])
    yb = _dot(ob_ref[...], wb_ref[...])
    o_ref[...] = (ga_ref[...].astype(F32) * ya + gb_ref[...].astype(F32) * yb).astype(o_ref.dtype)


def _merge(o_a, o_b, wa, wb, gate, tm=512, tn=512):
    m, ka = o_a.shape
    kb = o_b.shape[1]
    n = wa.shape[1]
    off = n // tn
    return pl.pallas_call(
        _merge_kernel,
        grid=(n // tn, m // tm),
        in_specs=[pl.BlockSpec((tm, ka), lambda j, i: (i, 0)),
                  pl.BlockSpec((tm, kb), lambda j, i: (i, 0)),
                  pl.BlockSpec((ka, tn), lambda j, i: (0, j)),
                  pl.BlockSpec((kb, tn), lambda j, i: (0, j)),
                  pl.BlockSpec((tm, tn), lambda j, i: (i, j)),
                  pl.BlockSpec((tm, tn), lambda j, i: (i, j + off))],
        out_specs=pl.BlockSpec((tm, tn), lambda j, i: (i, j)),
        out_shape=jax.ShapeDtypeStruct((m, n), BF16),
        compiler_params=_params(("parallel", "parallel")),
        name="branch_merge",
    )(o_a, o_b, wa, wb, gate, gate)


def _resid_norm_kernel(a_ref, b_ref, x_ref, gt_ref, g_ref, *rest, affine, emit_x):
    rest = list(rest)
    if affine:
        sc_ref, sh_ref = rest[0], rest[1]
        rest = rest[2:]
    if emit_x:
        xo_ref, no_ref, acc_ref = rest
    else:
        no_ref, acc_ref = rest
    kk = pl.program_id(1)

    @pl.when(kk == 0)
    def _():
        acc_ref[...] = jnp.zeros_like(acc_ref)

    acc_ref[...] += _dot(a_ref[...], b_ref[...])

    @pl.when(kk == pl.num_programs(1) - 1)
    def _():
        xn = x_ref[...] + gt_ref[...] * acc_ref[...]
        if emit_x:
            xo_ref[...] = xn
        ms = jnp.mean(xn * xn, axis=-1, keepdims=True)
        y = xn * lax.rsqrt(ms + NORM_EPS) * g_ref[...]
        if affine:
            y = y * (1.0 + sc_ref[...]) + sh_ref[...]
        no_ref[...] = y.astype(no_ref.dtype)


def _resid_norm(a, b, x, gt, g, sc=None, sh=None, *, rows_per_batch, norm_dtype, emit_x, tm=512, tk=2048,
                name="resid_norm"):
    m, k = a.shape
    n = b.shape[1]
    bsz = gt.shape[0]
    tpb = rows_per_batch // tm
    affine = sc is not None
    vec = lambda v: v.reshape(bsz, 1, n)
    in_specs = [pl.BlockSpec((tm, tk), lambda i, kk: (i, kk)),
                pl.BlockSpec((tk, n), lambda i, kk: (kk, 0)),
                pl.BlockSpec((tm, n), lambda i, kk: (i, 0)),
                pl.BlockSpec((None, 1, n), lambda i, kk: (i // tpb, 0, 0)),
                pl.BlockSpec((1, n), lambda i, kk: (0, 0))]
    args = [a, b, x, vec(gt), g.reshape(1, n)]
    if affine:
        in_specs += [pl.BlockSpec((None, 1, n), lambda i, kk: (i // tpb, 0, 0))] * 2
        args += [vec(sc), vec(sh)]
    out_specs = [pl.BlockSpec((tm, n), lambda i, kk: (i, 0))]
    out_shape = [jax.ShapeDtypeStruct((m, n), norm_dtype)]
    if emit_x:
        out_specs = [pl.BlockSpec((tm, n), lambda i, kk: (i, 0))] + out_specs
        out_shape = [jax.ShapeDtypeStruct((m, n), F32)] + out_shape
    return pl.pallas_call(
        functools.partial(_resid_norm_kernel, affine=affine, emit_x=emit_x),
        grid=(m // tm, k // tk),
        in_specs=in_specs,
        out_specs=out_specs,
        out_shape=out_shape,
        scratch_shapes=[pltpu.VMEM((tm, n), F32)],
        compiler_params=_params(("parallel", "arbitrary")),
        name=name,
    )(*args)


def _seg_sum(x, ones_bd):
    cols = x.shape[1]
    outs = [_dot3(_dot, x[:, c:c + LANES], ones_bd, True) for c in range(0, cols, LANES)]
    return outs[0] if len(outs) == 1 else jnp.concatenate(outs, axis=1)


def _inv_unit_lower(a_strict, ri, ci):
    n = a_strict.shape[0]
    eye = (ri == ci).astype(F32)
    same16 = (ri // 16) == (ci // 16)
    same32 = (ri // 32) == (ci // 32)
    a_d = jnp.where(same16, a_strict, 0.0)
    t = eye + a_d
    p = a_d.astype(BF16)
    for _ in range(3):
        p32 = _dot(p, p)
        p = p32.astype(BF16)
        t = t + _dot(t.astype(BF16), p)
    a_1 = jnp.where(same32 & (~same16), a_strict, 0.0).astype(BF16)
    tb = t.astype(BF16)
    t = t + _dot(_dot(tb, a_1).astype(BF16), tb)
    a_2 = jnp.where(~same32, a_strict, 0.0).astype(BF16)
    tb = t.astype(BF16)
    t = t + _dot(_dot(tb, a_2).astype(BF16), tb)
    del n
    return t


def _rwkv_kernel(pr_ref, pk_ref, pv_ref, sm_ref, mur_ref, muk_ref, muv_ref, w0_ref, w2_ref, a0_ref, a2_ref,
                 g2_ref, kk_ref, ka_ref, rk_ref, lnw_ref, lnb_ref, o_ref, s_ref, prev_ref, y_ref):
    tb, hw = pr_ref.shape
    n_chunk = tb // RW_CHUNK
    n_pair = hw // LANES
    c_len = RW_CHUNK

    @pl.when(pl.program_id(2) == 0)
    def _():
        s_ref[...] = jnp.zeros_like(s_ref)
        prev_ref[...] = jnp.zeros_like(prev_ref)

    row = lax.broadcasted_iota(jnp.int32, (tb, hw), 0)

    def shifted_mix(p_ref, mu_ref, slot):
        p = p_ref[...]
        ps = pltpu.roll(p, 1, 0)
        ps = jnp.where(row == 0, prev_ref[slot:slot + 1, :], ps)
        prev_ref[slot:slot + 1, :] = p[tb - 1:tb, :]
        return p + (ps - p) * mu_ref[...]

    r = shifted_mix(pr_ref, mur_ref, 0)
    k = shifted_mix(pk_ref, muk_ref, 1)
    v = shifted_mix(pv_ref, muv_ref, 2)

    lw = jnp.tanh(sm_ref[:, SM_W:SM_A]).astype(BF16)
    w_log = -_softplus(-(w0_ref[...] + _dot(lw, w2_ref[...]))) - 0.5
    logw = -jnp.exp(w_log)
    a = _sigmoid(a0_ref[...] + _dot(sm_ref[:, SM_A:SM_G].astype(BF16), a2_ref[...]))
    g = _dot(_sigmoid(sm_ref[:, SM_G:SM_DT]).astype(BF16), g2_ref[...])

    li = lax.broadcasted_iota(jnp.int32, (LANES, LANES), 0)
    lj = lax.broadcasted_iota(jnp.int32, (LANES, LANES), 1)
    same_head = (li // HEAD) == (lj // HEAD)
    ones_bd = same_head.astype(BF16)

    kkv = k * kk_ref[...]
    kkn = kkv / jnp.maximum(jnp.sqrt(_seg_sum(kkv * kkv, ones_bd)), 1e-12)
    k2 = k * (1.0 + (a - 1.0) * ka_ref[...])
    aa = -kkn
    bb = kkn * a
    bonus = _seg_sum(r * k2 * rk_ref[...], ones_bd) * v

    ti = lax.broadcasted_iota(jnp.int32, (tb, tb), 0)
    tj = lax.broadcasted_iota(jnp.int32, (tb, tb), 1)
    cum_mat = (((ti // c_len) == (tj // c_len)) & (tj <= ti)).astype(BF16)
    lc = _dot3(_dot, logw, cum_mat, False)
    e_pos = jnp.exp(lc)
    e_neg = jnp.exp(-lc)
    rt = r * e_pos
    at = aa * jnp.exp(lc - logw)
    bt = bb * e_neg
    kt = k2 * e_neg

    lane = lax.broadcasted_iota(jnp.int32, (c_len, LANES), 1)
    m0 = lane < HEAD

    def bd(x):
        return jnp.concatenate([jnp.where(m0, x, 0.0), jnp.where(m0, 0.0, x)], axis=0).astype(BF16)

    strict = same_head & ((li % HEAD) > (lj % HEAD))
    incl = same_head & ((li % HEAD) >= (lj % HEAD))

    for c in range(n_chunk):
        rs = slice(c * c_len, (c + 1) * c_len)
        for p in range(n_pair):
            ls = slice(p * LANES, (p + 1) * LANES)
            a_bd, r_bd, b_bd, k_bd, v_bd = bd(at[rs, ls]), bd(rt[rs, ls]), bd(bt[rs, ls]), bd(kt[rs, ls]), bd(v[rs, ls])
            gm = _dot_nt(jnp.concatenate([a_bd, r_bd], axis=0), jnp.concatenate([b_bd, k_bd], axis=0))
            a_ab = jnp.where(strict, gm[0:LANES, 0:LANES], 0.0)
            a_ak = jnp.where(strict, gm[0:LANES, LANES:], 0.0).astype(BF16)
            a_rb = jnp.where(incl, gm[LANES:, 0:LANES], 0.0).astype(BF16)
            a_rk = jnp.where(incl, gm[LANES:, LANES:], 0.0).astype(BF16)
            t_inv = _inv_unit_lower(a_ab, li, lj).astype(BF16)
            s_old = s_ref[p]
            s_b = s_old.astype(BF16)
            w0 = _dot_nt(a_bd, s_b) + _dot(a_ak, v_bd)
            u_bd = _dot(t_inv, w0.astype(BF16)).astype(BF16)
            y_bd = _dot_nt(r_bd, s_b) + _dot(a_rb, u_bd) + _dot(a_rk, v_bd)
            y_ref[rs, ls] = y_bd[0:c_len] + y_bd[c_len:]
            g_end = e_pos[(c + 1) * c_len - 1:(c + 1) * c_len, ls]
            s_ref[p] = g_end * (s_old + _dot_tn(u_bd, b_bd) + _dot_tn(v_bd, k_bd))

    y = y_ref[...]
    mu = _seg_sum(y, ones_bd) * (1.0 / HEAD)
    d = y - mu
    var = _seg_sum(d * d, ones_bd) * (1.0 / HEAD)
    yn = d * lax.rsqrt(var + RW_LNX_EPS) * lnw_ref[...] + lnb_ref[...]
    o_ref[...] = ((yn + bonus) * g).astype(o_ref.dtype)


def _rwkv_branch(p_rkv, small, mu_rkv, w0, w2, a0, a2, g2, k_k, k_a, r_k, lnx_w, lnx_b, tb=256, hw=256):
    bsz, t, w3 = p_rkv.shape
    w = w3 // 3
    nb = w // hw
    row = lambda v: v.reshape(1, -1).astype(F32)
    vec_spec = lambda off: pl.BlockSpec((1, hw), lambda b, gi, ti: (0, gi + off))
    rkv_spec = lambda off: pl.BlockSpec((None, tb, hw), lambda b, gi, ti: (b, ti, gi + off))
    mat_spec = lambda rows: pl.BlockSpec((rows, hw), lambda b, gi, ti: (0, gi))
    mu = row(mu_rkv)
    return pl.pallas_call(
        _rwkv_kernel,
        grid=(bsz, nb, t // tb),
        in_specs=[rkv_spec(0), rkv_spec(nb), rkv_spec(2 * nb),
                  pl.BlockSpec((None, tb, SM_WIDTH), lambda b, gi, ti: (b, ti, 0)),
                  vec_spec(0), vec_spec(nb), vec_spec(2 * nb),
                  vec_spec(0), mat_spec(LANES), vec_spec(0), mat_spec(LANES), mat_spec(RW_GATE_LORA),
                  vec_spec(0), vec_spec(0), vec_spec(0), vec_spec(0), vec_spec(0)],
        out_specs=pl.BlockSpec((None, tb, hw), lambda b, gi, ti: (b, ti, gi)),
        out_shape=jax.ShapeDtypeStruct((bsz, t, w), BF16),
        scratch_shapes=[pltpu.VMEM((hw // LANES, LANES, LANES), F32),
                        pltpu.VMEM((8, hw), F32),
                        pltpu.VMEM((tb, hw), F32)],
        compiler_params=_params(("parallel", "parallel", "arbitrary")),
        name="rwkv7_branch",
    )(p_rkv, p_rkv, p_rkv, small, mu, mu, mu, row(w0), w2, row(a0), a2, g2,
      row(k_k), row(k_a), row(r_k), row(lnx_w), row(lnx_b))


def _ssd_kernel(xbc_ref, z_ref, dt_ref, cw_ref, cb_ref, dtb_ref, alog_ref, dsk_ref, nw_ref, exp_ref,
                o_ref, ext_ref, act_ref, st_ref):
    ch, cw = xbc_ref.shape
    width = z_ref.shape[1]
    gw = width // MB_GROUPS
    tail = 8

    @pl.when(pl.program_id(1) == 0)
    def _():
        st_ref[...] = jnp.zeros_like(st_ref)
        ext_ref[0:tail, :] = jnp.zeros((tail, cw), F32)

    @pl.when(pl.program_id(1) != 0)
    def _():
        ext_ref[0:tail, :] = ext_ref[ch:ch + tail, :]

    ext_ref[tail:tail + ch, :] = xbc_ref[...]

    cblk = 512
    for c0 in range(0, cw, cblk):
        acc = cb_ref[:, c0:c0 + cblk] + cw_ref[0:1, c0:c0 + cblk] * ext_ref[tail - 3:tail - 3 + ch, c0:c0 + cblk]
        for kk in range(1, MB_CONV):
            acc = acc + cw_ref[kk:kk + 1, c0:c0 + cblk] * ext_ref[tail - 3 + kk:tail - 3 + kk + ch, c0:c0 + cblk]
        act_ref[:, c0:c0 + cblk] = acc * _sigmoid(acc)

    dt = _softplus(dt_ref[...] + dtb_ref[...])
    d_a = dt * (-jnp.exp(alog_ref[...]))
    ti = lax.broadcasted_iota(jnp.int32, (ch, ch), 0)
    tj = lax.broadcasted_iota(jnp.int32, (ch, ch), 1)
    causal = tj <= ti
    acs = _dot3(_dot, d_a, causal.astype(BF16), False)
    acs_t = _dot3(_dot_tn, d_a, (ti <= tj).astype(BF16), True)
    expand = exp_ref[...]
    dt_x = _dot3(_dot, dt, expand, True)
    acs_x = _dot3(_dot, acs, expand, True)
    acs_last_x = acs_x[ch - 1:ch, :]
    lane = lax.broadcasted_iota(jnp.int32, (ch, LANES), 1)
    m0 = lane < HEAD

    for gi in range(MB_GROUPS):
        gs = slice(gi * gw, (gi + 1) * gw)
        c_g = act_ref[:, width + MB_GROUPS * MB_STATE + gi * MB_STATE:width + MB_GROUPS * MB_STATE + (gi + 1) * MB_STATE].astype(BF16)
        b_g = act_ref[:, width + gi * MB_STATE:width + (gi + 1) * MB_STATE].astype(BF16)
        xs_g = act_ref[:, gs]
        acs_g = acs_x[:, gs]
        x_g = xs_g * dt_x[:, gs]
        cb = _dot_nt(c_g, b_g)
        s_old = st_ref[gi]
        y_off = _dot(c_g, s_old.astype(BF16)) * jnp.exp(acs_g)
        y_diag = []
        for q in range(gw // LANES):
            x_pair = x_g[:, q * LANES:(q + 1) * LANES]
            y_pair = None
            for hh in range(2):
                head = (gi * gw) // HEAD + 2 * q + hh
                col = jnp.broadcast_to(acs[:, head:head + 1], (ch, ch))
                rowv = acs_t[head:head + 1, :]
                l_dec = jnp.exp(jnp.where(causal, col - rowv, -jnp.inf))
                scores = (cb * l_dec).astype(BF16)
                x_m = (jnp.where(m0, x_pair, 0.0) if hh == 0 else jnp.where(m0, 0.0, x_pair)).astype(BF16)
                part = _dot(scores, x_m)
                y_pair = part if y_pair is None else y_pair + part
            y_diag.append(y_pair)
        y = jnp.concatenate(y_diag, axis=1) + y_off + xs_g * dsk_ref[:, gs]
        last_g = acs_last_x[:, gs]
        x_dec = (x_g * jnp.exp(last_g - acs_g)).astype(BF16)
        st_ref[gi] = s_old * jnp.exp(last_g) + _dot_tn(b_g, x_dec)
        z = z_ref[:, gs]
        yg = y * (z * _sigmoid(z))
        ms = jnp.mean(yg * yg, axis=-1, keepdims=True)
        o_ref[:, gs] = (yg * lax.rsqrt(ms + NORM_EPS) * nw_ref[:, gs]).astype(o_ref.dtype)


def _ssd_branch(p_z, p_xbc, small, conv_w, conv_b, dt_bias, a_log, d_skip, norm_w):
    bsz, t, width = p_z.shape
    cw = p_xbc.shape[2]
    heads = width // HEAD
    ch = MB_CHUNK
    pad = lambda v: jnp.zeros((1, LANES), F32).at[0, :heads].set(v.astype(F32))
    hid = jnp.arange(LANES)[:, None]
    col = jnp.arange(width)[None, :]
    expand = (hid == col // HEAD).astype(BF16)
    d_x = jnp.repeat(d_skip.astype(F32), HEAD).reshape(1, width)
    full = lambda shape: pl.BlockSpec(shape, lambda b, c: (0, 0))
    return pl.pallas_call(
        _ssd_kernel,
        grid=(bsz, t // ch),
        in_specs=[pl.BlockSpec((None, ch, cw), lambda b, c: (b, c, 0)),
                  pl.BlockSpec((None, ch, width), lambda b, c: (b, c, 0)),
                  pl.BlockSpec((None, ch, LANES), lambda b, c: (b, c, SM_DT // LANES)),
                  full((MB_CONV, cw)), full((1, cw)), full((1, LANES)), full((1, LANES)),
                  full((1, width)), full((1, width)), full((LANES, width))],
        out_specs=pl.BlockSpec((None, ch, width), lambda b, c: (b, c, 0)),
        out_shape=jax.ShapeDtypeStruct((bsz, t, width), BF16),
        scratch_shapes=[pltpu.VMEM((ch + 8, cw), F32),
                        pltpu.VMEM((ch, cw), F32),
                        pltpu.VMEM((MB_GROUPS, MB_STATE, width // MB_GROUPS), F32)],
        compiler_params=_params(("parallel", "arbitrary")),
        name="mamba2_branch",
    )(p_xbc, p_z, small, conv_w.astype(F32), conv_b.reshape(1, cw).astype(F32), pad(dt_bias), pad(a_log),
      d_x, norm_w.reshape(1, width).astype(F32), expand)


def _pad_cols(w, n):
    return jnp.zeros((w.shape[0], n), w.dtype).at[:, :w.shape[1]].set(w)


def _pad_rows(w, n):
    return jnp.zeros((n, w.shape[1]), w.dtype).at[:w.shape[0]].set(w)


def _layer(x, c_act_mod, l, norm1_g, w_in, rw_mu_rkv, rw_mu_wag, rw_w0, rw_w1, rw_w2, rw_a0, rw_a1, rw_a2, rw_g1,
           rw_g2, rw_k_k, rw_k_a, rw_r_k, rw_lnx_w, rw_lnx_b, mb_conv_w, mb_conv_b, mb_dt_bias, mb_a_log, mb_d,
           mb_norm_w, w_branch_a, w_branch_b, w_gate, b_gate, w_out, norm2_g, mlp_up, mlp_down, out_g, out_affine):
    bsz, t, d = x.shape
    m = bsz * t
    w = rw_k_k.shape[1]
    mbw = mb_norm_w.shape[1]
    cw = mb_conv_b.shape[1]
    off_rkv, off_z, off_xbc = 3 * w, 3 * w + mbw, 3 * w + mbw + cw
    sh1, sc1, gt1, sh2, sc2, gt2 = jnp.split(c_act_mod, 6, axis=-1)

    w_small = jnp.concatenate([_pad_cols(rw_w1[l], LANES), _pad_cols(rw_a1[l], LANES), rw_g1[l],
                               _pad_cols(w_in[l][:, off_xbc:], LANES)], axis=1).astype(BF16)
    h, small = _norm1(x, norm1_g[l], sc1, sh1, rw_mu_wag[l], w_small)
    h2d = h.reshape(m, d)
    wi = w_in[l].astype(BF16)
    p_rkv = _matmul(h2d, wi[:, :off_rkv], F32, name="proj_rkv").reshape(bsz, t, off_rkv)
    p_z = _matmul(h2d, wi[:, off_rkv:off_z], F32, name="proj_z").reshape(bsz, t, mbw)
    p_xbc = _matmul(h2d, wi[:, off_z:off_xbc], F32, name="proj_xbc").reshape(bsz, t, cw)
    gate = _matmul(h2d, w_gate[l].astype(BF16), BF16, bias=b_gate[l], act="sigmoid", name="proj_gate")

    o_a = _rwkv_branch(p_rkv, small, rw_mu_rkv[l], rw_w0[l], _pad_rows(rw_w2[l], LANES).astype(BF16), rw_a0[l],
                       _pad_rows(rw_a2[l], LANES).astype(BF16), rw_g2[l].astype(BF16), rw_k_k[l], rw_k_a[l],
                       rw_r_k[l], rw_lnx_w[l], rw_lnx_b[l])
    o_b = _ssd_branch(p_z, p_xbc, small, mb_conv_w[l], mb_conv_b[l], mb_dt_bias[l], mb_a_log[l], mb_d[l],
                      mb_norm_w[l])
    merged = _merge(o_a.reshape(m, w), o_b.reshape(m, mbw), w_branch_a[l].astype(BF16),
                    w_branch_b[l].astype(BF16), gate)
    x1, h2 = _resid_norm(merged, w_out[l].astype(BF16), x.reshape(m, d), gt1, norm2_g[l], sc2, sh2,
                         rows_per_batch=t, norm_dtype=BF16, emit_x=True, name="out_proj_norm2")
    u = _matmul(h2, mlp_up[l].astype(BF16), BF16, act="relu2", name="mlp_up")
    if out_affine is None:
        (y,) = _resid_norm(u, mlp_down[l].astype(BF16), x1, gt2, out_g, rows_per_batch=t, norm_dtype=F32,
                           emit_x=False, name="mlp_down_final_norm")
        return y.reshape(bsz, t, d)
    raise NotImplementedError("only a single layer followed by the final norm is implemented")


def kernel(x, c, ada_w, ada_b, norm1_g, w_in, rw_mu_rkv, rw_mu_wag, rw_w0, rw_w1, rw_w2, rw_a0, rw_a1, rw_a2, rw_g1, rw_g2, rw_k_k, rw_k_a, rw_r_k, rw_lnx_w, rw_lnx_b, mb_conv_w, mb_conv_b, mb_dt_bias, mb_a_log, mb_d, mb_norm_w, w_branch_a, w_branch_b, w_gate, b_gate, w_out, norm2_g, mlp_up, mlp_down, final_norm_g):
    depth = ada_w.shape[0]
    assert depth == 1, "the fused final-norm epilogue assumes a single layer"
    mod = _ada_mod(c, ada_w[0], ada_b[0])
    return _layer(x, mod, 0, norm1_g, w_in, rw_mu_rkv, rw_mu_wag, rw_w0, rw_w1, rw_w2, rw_a0, rw_a1, rw_a2, rw_g1,
                  rw_g2, rw_k_k, rw_k_a, rw_r_k, rw_lnx_w, rw_lnx_b, mb_conv_w, mb_conv_b, mb_dt_bias, mb_a_log,
                  mb_d, mb_norm_w, w_branch_a, w_branch_b, w_gate, b_gate, w_out, norm2_g, mlp_up, mlp_down,
                  final_norm_g, None)
```

```python
import functools

import jax
import jax.numpy as jnp
from jax import lax
from jax.experimental import pallas as pl
from jax.experimental.pallas import tpu as pltpu

F32 = jnp.float32
BF16 = jnp.bfloat16

LANES = 128
HEAD = 64
RW_CHUNK = 64
RW_DECAY_LORA = 96
RW_AAA_LORA = 96
RW_GATE_LORA = 256
RW_LNX_EPS = 64e-5
MB_GROUPS = 8
MB_STATE = 128
MB_CONV = 4
MB_CHUNK = 128
NORM_EPS = 1e-5
VMEM_LIMIT = 56 * 1024 * 1024

SM_W, SM_A, SM_G, SM_DT, SM_WIDTH = 0, 128, 256, 512, 640


def _params(sem):
    return pltpu.CompilerParams(dimension_semantics=sem, vmem_limit_bytes=VMEM_LIMIT)


def _sigmoid(x):
    return 1.0 / (1.0 + jnp.exp(-x))


def _softplus(x):
    return jnp.maximum(x, 0.0) + jnp.log1p(jnp.exp(-jnp.abs(x)))


def _dot(a, b):
    return jnp.dot(a, b, preferred_element_type=F32)


def _dot_nt(a, b):
    return lax.dot_general(a, b, (((1,), (1,)), ((), ())), preferred_element_type=F32)


def _dot_tn(a, b):
    return lax.dot_general(a, b, (((0,), (0,)), ((), ())), preferred_element_type=F32)


def _split_bf16(x, terms):
    parts = []
    for _ in range(terms - 1):
        hi = x.astype(BF16)
        parts.append(hi)
        x = x - hi.astype(F32)
    parts.append(x.astype(BF16))
    return parts


def _dot_split(dot_fn, x, m, x_is_lhs, terms=3):
    parts = _split_bf16(x, terms)
    prods = [dot_fn(p, m) if x_is_lhs else dot_fn(m, p) for p in parts]
    out = prods[0]
    for p in prods[1:]:
        out = out + p
    return out


def _ada_kernel(c_ref, w_ref, b_ref, o_ref):
    c = c_ref[...]
    ca = (c * _sigmoid(c)).astype(BF16)
    o_ref[...] = _dot(ca, w_ref[...].astype(BF16)) + b_ref[...]


def _ada_mod(c, ada_w, ada_b):
    bsz, d = c.shape
    n = ada_w.shape[1]
    rows = 8
    c_pad = jnp.zeros((rows, d), F32).at[:bsz].set(c)
    tn = 1024
    out = pl.pallas_call(
        _ada_kernel,
        grid=(n // tn,),
        in_specs=[pl.BlockSpec((rows, d), lambda j: (0, 0)),
                  pl.BlockSpec((d, tn), lambda j: (0, j)),
                  pl.BlockSpec((1, tn), lambda j: (0, j))],
        out_specs=pl.BlockSpec((rows, tn), lambda j: (0, j)),
        out_shape=jax.ShapeDtypeStruct((rows, n), F32),
        compiler_params=_params(("parallel",)),
        name="ada_mod",
    )(c_pad, ada_w, ada_b.reshape(1, n))
    return out[:bsz]


def _norm1_kernel(x_ref, g_ref, sc_ref, sh_ref, mu_ref, ws_ref, h_ref, sm_ref, prev_ref):
    i = pl.program_id(1)

    @pl.when(i == 0)
    def _():
        prev_ref[...] = jnp.zeros_like(prev_ref)

    x = x_ref[...]
    tm = x.shape[0]
    ms = jnp.mean(x * x, axis=-1, keepdims=True)
    h = x * lax.rsqrt(ms + NORM_EPS) * g_ref[...] * (1.0 + sc_ref[...]) + sh_ref[...]
    hs = pltpu.roll(h, 1, 0)
    row = lax.broadcasted_iota(jnp.int32, h.shape, 0)
    hs = jnp.where(row == 0, prev_ref[0:1, :], hs)
    prev_ref[0:1, :] = h[tm - 1:tm, :]
    dh = hs - h
    hb = h.astype(BF16)
    h_ref[...] = hb
    xw = (h + dh * mu_ref[0:1, :]).astype(BF16)
    sm_ref[:, SM_W:SM_A] = _dot(xw, ws_ref[:, SM_W:SM_A])
    xa = (h + dh * mu_ref[1:2, :]).astype(BF16)
    sm_ref[:, SM_A:SM_G] = _dot(xa, ws_ref[:, SM_A:SM_G])
    xg = (h + dh * mu_ref[2:3, :]).astype(BF16)
    sm_ref[:, SM_G:SM_DT] = _dot(xg, ws_ref[:, SM_G:SM_DT])
    sm_ref[:, SM_DT:SM_WIDTH] = _dot(hb, ws_ref[:, SM_DT:SM_WIDTH])


def _norm1(x, g, sc, sh, mu_wag, w_small, tm=256):
    bsz, t, d = x.shape
    return pl.pallas_call(
        _norm1_kernel,
        grid=(bsz, t // tm),
        in_specs=[pl.BlockSpec((None, tm, d), lambda b, i: (b, i, 0)),
                  pl.BlockSpec((1, d), lambda b, i: (0, 0)),
                  pl.BlockSpec((None, 1, d), lambda b, i: (b, 0, 0)),
                  pl.BlockSpec((None, 1, d), lambda b, i: (b, 0, 0)),
                  pl.BlockSpec((3, d), lambda b, i: (0, 0)),
                  pl.BlockSpec((d, SM_WIDTH), lambda b, i: (0, 0))],
        out_specs=[pl.BlockSpec((None, tm, d), lambda b, i: (b, i, 0)),
                   pl.BlockSpec((None, tm, SM_WIDTH), lambda b, i: (b, i, 0))],
        out_shape=[jax.ShapeDtypeStruct((bsz, t, d), BF16),
                   jax.ShapeDtypeStruct((bsz, t, SM_WIDTH), F32)],
        scratch_shapes=[pltpu.VMEM((8, d), F32)],
        compiler_params=_params(("parallel", "arbitrary")),
        name="norm1_small_proj",
    )(x, g.reshape(1, d), sc.reshape(bsz, 1, d), sh.reshape(bsz, 1, d), mu_wag, w_small)


def _mm_kernel(a_ref, b_ref, *rest, act, has_bias):
    if has_bias:
        bias_ref, o_ref = rest
    else:
        (o_ref,) = rest
    y = _dot(a_ref[...], b_ref[...])
    if has_bias:
        y = y + bias_ref[...]
    if act == "sigmoid":
        y = _sigmoid(y)
    elif act == "relu2":
        y = jnp.maximum(y, 0.0)
        y = y * y
    o_ref[...] = y.astype(o_ref.dtype)


def _matmul(a, b, out_dtype, bias=None, act=None, tm=1024, tn=1024, name="matmul"):
    m, k = a.shape
    n = b.shape[1]
    tm, tn = min(tm, m), min(tn, n)
    in_specs = [pl.BlockSpec((tm, k), lambda j, i: (i, 0)),
                pl.BlockSpec((k, tn), lambda j, i: (0, j))]
    args = [a, b]
    if bias is not None:
        in_specs.append(pl.BlockSpec((1, tn), lambda j, i: (0, j)))
        args.append(bias.reshape(1, n).astype(F32))
    return pl.pallas_call(
        functools.partial(_mm_kernel, act=act, has_bias=bias is not None),
        grid=(n // tn, m // tm),
        in_specs=in_specs,
        out_specs=pl.BlockSpec((tm, tn), lambda j, i: (i, j)),
        out_shape=jax.ShapeDtypeStruct((m, n), out_dtype),
        compiler_params=_params(("parallel", "parallel")),
        name=name,
    )(*args)


def _merge_kernel(oa_ref, ob_ref, wa_ref, wb_ref, ga_ref, gb_ref, o_ref):
    ya = _dot(oa_ref[...], wa_ref[...])
    yb = _dot(ob_ref[...], wb_ref[...])
    o_ref[...] = (ga_ref[...].astype(F32) * ya + gb_ref[...].astype(F32) * yb).astype(o_ref.dtype)


def _merge(o_a, o_b, wa, wb, gate, tm=512, tn=512):
    m, ka = o_a.shape
    kb = o_b.shape[1]
    n = wa.shape[1]
    off = n // tn
    return pl.pallas_call(
        _merge_kernel,
        grid=(n // tn, m // tm),
        in_specs=[pl.BlockSpec((tm, ka), lambda j, i: (i, 0)),
                  pl.BlockSpec((tm, kb), lambda j, i: (i, 0)),
                  pl.BlockSpec((ka, tn), lambda j, i: (0, j)),
                  pl.BlockSpec((kb, tn), lambda j, i: (0, j)),
                  pl.BlockSpec((tm, tn), lambda j, i: (i, j)),
                  pl.BlockSpec((tm, tn), lambda j, i: (i, j + off))],
        out_specs=pl.BlockSpec((tm, tn), lambda j, i: (i, j)),
        out_shape=jax.ShapeDtypeStruct((m, n), BF16),
        compiler_params=_params(("parallel", "parallel")),
        name="branch_merge",
    )(o_a, o_b, wa, wb, gate, gate)


def _resid_norm_kernel(a_ref, b_ref, x_ref, gt_ref, g_ref, *rest, affine, emit_x):
    rest = list(rest)
    if affine:
        sc_ref, sh_ref = rest[0], rest[1]
        rest = rest[2:]
    if emit_x:
        xo_ref, no_ref, acc_ref = rest
    else:
        no_ref, acc_ref = rest
    kk = pl.program_id(1)

    @pl.when(kk == 0)
    def _():
        acc_ref[...] = jnp.zeros_like(acc_ref)

    acc_ref[...] += _dot(a_ref[...], b_ref[...])

    @pl.when(kk == pl.num_programs(1) - 1)
    def _():
        xn = x_ref[...] + gt_ref[...] * acc_ref[...]
        if emit_x:
            xo_ref[...] = xn
        ms = jnp.mean(xn * xn, axis=-1, keepdims=True)
        y = xn * lax.rsqrt(ms + NORM_EPS) * g_ref[...]
        if affine:
            y = y * (1.0 + sc_ref[...]) + sh_ref[...]
        no_ref[...] = y.astype(no_ref.dtype)


def _resid_norm(a, b, x, gt, g, sc=None, sh=None, *, rows_per_batch, norm_dtype, emit_x, tm=512, tk=2048,
                name="resid_norm"):
    m, k = a.shape
    n = b.shape[1]
    bsz = gt.shape[0]
    tpb = rows_per_batch // tm
    affine = sc is not None
    vec = lambda v: v.reshape(bsz, 1, n)
    in_specs = [pl.BlockSpec((tm, tk), lambda i, kk: (i, kk)),
                pl.BlockSpec((tk, n), lambda i, kk: (kk, 0)),
                pl.BlockSpec((tm, n), lambda i, kk: (i, 0)),
                pl.BlockSpec((None, 1, n), lambda i, kk: (i // tpb, 0, 0)),
                pl.BlockSpec((1, n), lambda i, kk: (0, 0))]
    args = [a, b, x, vec(gt), g.reshape(1, n)]
    if affine:
        in_specs += [pl.BlockSpec((None, 1, n), lambda i, kk: (i // tpb, 0, 0))] * 2
        args += [vec(sc), vec(sh)]
    out_specs = [pl.BlockSpec((tm, n), lambda i, kk: (i, 0))]
    out_shape = [jax.ShapeDtypeStruct((m, n), norm_dtype)]
    if emit_x:
        out_specs = [pl.BlockSpec((tm, n), lambda i, kk: (i, 0))] + out_specs
        out_shape = [jax.ShapeDtypeStruct((m, n), F32)] + out_shape
    return pl.pallas_call(
        functools.partial(_resid_norm_kernel, affine=affine, emit_x=emit_x),
        grid=(m // tm, k // tk),
        in_specs=in_specs,
        out_specs=out_specs,
        out_shape=out_shape,
        scratch_shapes=[pltpu.VMEM((tm, n), F32)],
        compiler_params=_params(("parallel", "arbitrary")),
        name=name,
    )(*args)


def _seg_sum(x, ones_bd):
    cols = x.shape[1]
    wid = ones_bd.shape[0]
    outs = [_dot_split(_dot, x[:, c:c + wid], ones_bd, True, terms=2) for c in range(0, cols, wid)]
    return outs[0] if len(outs) == 1 else jnp.concatenate(outs, axis=1)


def _inv_unit_lower_all(a_list, ri, ci):
    n = a_list[0].shape[0]
    eye = (ri == ci).astype(F32)
    same16 = (ri // 16) == (ci // 16)
    same32 = (ri // 32) == (ci // 32)
    t = [eye + jnp.where(same16, a, 0.0) for a in a_list]
    p = [jnp.where(same16, a, 0.0).astype(BF16) for a in a_list]
    p = [_dot(x, x).astype(BF16) for x in p]
    for _ in range(2):
        r = [_dot(x, jnp.concatenate([tt.astype(BF16), x], axis=1)) for x, tt in zip(p, t)]
        t = [tt + rr[:, :n] for tt, rr in zip(t, r)]
        p = [rr[:, n:].astype(BF16) for rr in r]
    t = [tt + _dot(x, tt.astype(BF16)) for x, tt in zip(p, t)]
    for half, sel in ((16, same32 & (~same16)), (32, ~same32)):
        lo = [s for s in range(0, n, half) if (s // half) % 2 == 1]
        tb = [tt.astype(BF16) for tt in t]
        t_lo = [jnp.concatenate([tt[s:s + half] for s in lo], axis=0).astype(BF16) for tt in t]
        x = [_dot(tl, jnp.where(sel, a, 0.0).astype(BF16)).astype(BF16) for tl, a in zip(t_lo, a_list)]
        upd = [_dot(xx, b) for xx, b in zip(x, tb)]

        def add_rows(tt, u):
            parts = [tt[s:s + half] for s in range(0, n, half)]
            for j, s in enumerate(lo):
                parts[s // half] = parts[s // half] + u[j * half:(j + 1) * half]
            return jnp.concatenate(parts, axis=0)

        t = [add_rows(tt, u) for tt, u in zip(t, upd)]
    return t


def _rwkv_kernel(pr_ref, pk_ref, pv_ref, sm_ref, mur_ref, muk_ref, muv_ref, w0_ref, w2_ref, a0_ref, a2_ref,
                 g2_ref, kk_ref, ka_ref, rk_ref, lnw_ref, lnb_ref, o_ref, s_ref, prev_ref, y_ref):
    tb, hw = pr_ref.shape
    n_chunk = tb // RW_CHUNK
    n_pair = hw // LANES
    c_len = RW_CHUNK

    @pl.when(pl.program_id(2) == 0)
    def _():
        s_ref[...] = jnp.zeros_like(s_ref)
        prev_ref[...] = jnp.zeros_like(prev_ref)

    row = lax.broadcasted_iota(jnp.int32, (tb, hw), 0)

    def shifted_mix(p_ref, mu_ref, slot):
        p = p_ref[...]
        ps = pltpu.roll(p, 1, 0)
        ps = jnp.where(row == 0, prev_ref[slot:slot + 1, :], ps)
        prev_ref[slot:slot + 1, :] = p[tb - 1:tb, :]
        return p + (ps - p) * mu_ref[...]

    r = shifted_mix(pr_ref, mur_ref, 0)
    k = shifted_mix(pk_ref, muk_ref, 1)
    v = shifted_mix(pv_ref, muv_ref, 2)

    lw = jnp.tanh(sm_ref[:, SM_W:SM_A]).astype(BF16)
    w_log = -_softplus(-(w0_ref[...] + _dot(lw, w2_ref[...]))) - 0.5
    logw = -jnp.exp(w_log)
    a = _sigmoid(a0_ref[...] + _dot(sm_ref[:, SM_A:SM_G].astype(BF16), a2_ref[...]))
    g = _dot(_sigmoid(sm_ref[:, SM_G:SM_DT]).astype(BF16), g2_ref[...])

    li = lax.broadcasted_iota(jnp.int32, (LANES, LANES), 0)
    lj = lax.broadcasted_iota(jnp.int32, (LANES, LANES), 1)
    same_head = (li // HEAD) == (lj // HEAD)
    seg_w = min(2 * LANES, hw)
    si = lax.broadcasted_iota(jnp.int32, (seg_w, seg_w), 0)
    sj = lax.broadcasted_iota(jnp.int32, (seg_w, seg_w), 1)
    ones_bd = ((si // HEAD) == (sj // HEAD)).astype(BF16)

    kkv = k * kk_ref[...]
    kkn = kkv / jnp.maximum(jnp.sqrt(_seg_sum(kkv * kkv, ones_bd)), 1e-12)
    k2 = k * (1.0 + (a - 1.0) * ka_ref[...])
    aa = -kkn
    bb = kkn * a
    bonus = _seg_sum(r * k2 * rk_ref[...], ones_bd) * v

    ti = lax.broadcasted_iota(jnp.int32, (tb, tb), 0)
    tj = lax.broadcasted_iota(jnp.int32, (tb, tb), 1)
    cum_mat = (((ti // c_len) == (tj // c_len)) & (tj <= ti)).astype(BF16)
    lc = _dot_split(_dot, logw, cum_mat, False)
    e_pos = jnp.exp(lc)
    e_neg = jnp.exp(-lc)
    rt = r * e_pos
    at = aa * jnp.exp(lc - logw)
    bt = bb * e_neg
    kt = k2 * e_neg

    lane = lax.broadcasted_iota(jnp.int32, (c_len, LANES), 1)
    m0 = lane < HEAD

    def bd(x):
        return jnp.concatenate([jnp.where(m0, x, 0.0), jnp.where(m0, 0.0, x)], axis=0).astype(BF16)

    strict = same_head & ((li % HEAD) > (lj % HEAD))
    incl = same_head & ((li % HEAD) >= (lj % HEAD))

    inst = [(c, p) for c in range(n_chunk) for p in range(n_pair)]
    rows = lambda c: slice(c * c_len, (c + 1) * c_len)
    lanes = lambda p: slice(p * LANES, (p + 1) * LANES)
    tile = lambda x: [bd(x[rows(c), lanes(p)]) for c, p in inst]
    a_bd, r_bd, b_bd, k_bd, v_bd = tile(at), tile(rt), tile(bt), tile(kt), tile(v)

    bk_bd = [jnp.concatenate([z, w], axis=0) for z, w in zip(b_bd, k_bd)]
    gm = [_dot_nt(jnp.concatenate([x, y], axis=0), zw) for x, y, zw in zip(a_bd, r_bd, bk_bd)]
    a_ab = [jnp.where(strict, x[0:LANES, 0:LANES], 0.0) for x in gm]
    a_ak = [jnp.where(strict, x[0:LANES, LANES:], 0.0).astype(BF16) for x in gm]
    a_rb = [jnp.where(incl, x[LANES:, 0:LANES], 0.0).astype(BF16) for x in gm]
    a_rk = [jnp.where(incl, x[LANES:, LANES:], 0.0).astype(BF16) for x in gm]
    t_inv = [x.astype(BF16) for x in _inv_unit_lower_all(a_ab, li, lj)]
    av = [_dot(x, y).astype(BF16) for x, y in zip(a_ak, v_bd)]
    tav = [_dot(t, jnp.concatenate([x, y], axis=1)) for t, x, y in zip(t_inv, a_bd, av)]
    ta = [x[:, :LANES].astype(BF16) for x in tav]
    uv = [x[:, LANES:] for x in tav]
    yv = [_dot(x, y) for x, y in zip(a_rk, v_bd)]

    state = [s_ref[p] for p in range(n_pair)]
    for c in range(n_chunk):
        ids = [c * n_pair + p for p in range(n_pair)]
        s_b = [s.astype(BF16) for s in state]
        u_bd = [(_dot_nt(ta[i], s_b[p]) + uv[i]).astype(BF16) for p, i in enumerate(ids)]
        g_end = [e_pos[(c + 1) * c_len - 1:(c + 1) * c_len, lanes(p)] for p in range(n_pair)]
        state = [g_end[p] * (state[p] + _dot_tn(jnp.concatenate([u_bd[p], v_bd[i]], axis=0), bk_bd[i]))
                 for p, i in enumerate(ids)]
        for p, i in enumerate(ids):
            y_bd = _dot_nt(r_bd[i], s_b[p]) + _dot(a_rb[i], u_bd[p]) + yv[i]
            y_ref[rows(c), lanes(p)] = y_bd[0:c_len] + y_bd[c_len:]
    for p in range(n_pair):
        s_ref[p] = state[p]

    y = y_ref[...]
    mu = _seg_sum(y, ones_bd) * (1.0 / HEAD)
    d = y - mu
    var = _seg_sum(d * d, ones_bd) * (1.0 / HEAD)
    yn = d * lax.rsqrt(var + RW_LNX_EPS) * lnw_ref[...] + lnb_ref[...]
    o_ref[...] = ((yn + bonus) * g).astype(o_ref.dtype)


def _rwkv_branch(p_rkv, small, mu_rkv, w0, w2, a0, a2, g2, k_k, k_a, r_k, lnx_w, lnx_b, tb=256, hw=1024):
    bsz, t, w3 = p_rkv.shape
    w = w3 // 3
    nb = w // hw
    row = lambda v: v.reshape(1, -1).astype(F32)
    vec_spec = lambda off: pl.BlockSpec((1, hw), lambda b, gi, ti: (0, gi + off))
    rkv_spec = lambda off: pl.BlockSpec((None, tb, hw), lambda b, gi, ti: (b, ti, gi + off))
    mat_spec = lambda rows: pl.BlockSpec((rows, hw), lambda b, gi, ti: (0, gi))
    mu = row(mu_rkv)
    return pl.pallas_call(
        _rwkv_kernel,
        grid=(bsz, nb, t // tb),
        in_specs=[rkv_spec(0), rkv_spec(nb), rkv_spec(2 * nb),
                  pl.BlockSpec((None, tb, SM_WIDTH), lambda b, gi, ti: (b, ti, 0)),
                  vec_spec(0), vec_spec(nb), vec_spec(2 * nb),
                  vec_spec(0), mat_spec(LANES), vec_spec(0), mat_spec(LANES), mat_spec(RW_GATE_LORA),
                  vec_spec(0), vec_spec(0), vec_spec(0), vec_spec(0), vec_spec(0)],
        out_specs=pl.BlockSpec((None, tb, hw), lambda b, gi, ti: (b, ti, gi)),
        out_shape=jax.ShapeDtypeStruct((bsz, t, w), BF16),
        scratch_shapes=[pltpu.VMEM((hw // LANES, LANES, LANES), F32),
                        pltpu.VMEM((8, hw), F32),
                        pltpu.VMEM((tb, hw), F32)],
        compiler_params=_params(("parallel", "parallel", "arbitrary")),
        name="rwkv7_branch",
    )(p_rkv, p_rkv, p_rkv, small, mu, mu, mu, row(w0), w2, row(a0), a2, g2,
      row(k_k), row(k_a), row(r_k), row(lnx_w), row(lnx_b))


def _ssd_kernel(xbc_ref, z_ref, dt_ref, cw_ref, cb_ref, dtb_ref, alog_ref, dsk_ref, nw_ref, exp_ref,
                o_ref, ext_ref, act_ref, st_ref):
    ch, cw = xbc_ref.shape
    width = z_ref.shape[1]
    gw = width // MB_GROUPS
    tail = 8

    @pl.when(pl.program_id(1) == 0)
    def _():
        st_ref[...] = jnp.zeros_like(st_ref)
        ext_ref[0:tail, :] = jnp.zeros((tail, cw), F32)

    @pl.when(pl.program_id(1) != 0)
    def _():
        ext_ref[0:tail, :] = ext_ref[ch:ch + tail, :]

    ext_ref[tail:tail + ch, :] = xbc_ref[...]

    cblk = 512
    for c0 in range(0, cw, cblk):
        acc = cb_ref[:, c0:c0 + cblk] + cw_ref[0:1, c0:c0 + cblk] * ext_ref[tail - 3:tail - 3 + ch, c0:c0 + cblk]
        for kk in range(1, MB_CONV):
            acc = acc + cw_ref[kk:kk + 1, c0:c0 + cblk] * ext_ref[tail - 3 + kk:tail - 3 + kk + ch, c0:c0 + cblk]
        act_ref[:, c0:c0 + cblk] = acc * _sigmoid(acc)

    dt = _softplus(dt_ref[...] + dtb_ref[...])
    d_a = dt * (-jnp.exp(alog_ref[...]))
    ti = lax.broadcasted_iota(jnp.int32, (ch, ch), 0)
    tj = lax.broadcasted_iota(jnp.int32, (ch, ch), 1)
    causal = tj <= ti
    acs = _dot_split(_dot, d_a, causal.astype(BF16), False)
    acs_t = _dot_split(_dot_tn, d_a, (ti <= tj).astype(BF16), True)
    expand = exp_ref[...]
    dt_x = _dot_split(_dot, dt, expand, True)
    acs_x = _dot_split(_dot, acs, expand, True)
    acs_last_x = acs_x[ch - 1:ch, :]
    lane = lax.broadcasted_iota(jnp.int32, (ch, LANES), 1)
    m0 = lane < HEAD

    for gi in range(MB_GROUPS):
        gs = slice(gi * gw, (gi + 1) * gw)
        c_g = act_ref[:, width + MB_GROUPS * MB_STATE + gi * MB_STATE:width + MB_GROUPS * MB_STATE + (gi + 1) * MB_STATE].astype(BF16)
        b_g = act_ref[:, width + gi * MB_STATE:width + (gi + 1) * MB_STATE].astype(BF16)
        xs_g = act_ref[:, gs]
        acs_g = acs_x[:, gs]
        x_g = xs_g * dt_x[:, gs]
        cb = _dot_nt(c_g, b_g)
        s_old = st_ref[gi]
        y_off = _dot(c_g, s_old.astype(BF16)) * jnp.exp(acs_g)
        y_diag = []
        for q in range(gw // LANES):
            x_pair = x_g[:, q * LANES:(q + 1) * LANES]
            y_pair = None
            for hh in range(2):
                head = (gi * gw) // HEAD + 2 * q + hh
                col = jnp.broadcast_to(acs[:, head:head + 1], (ch, ch))
                rowv = acs_t[head:head + 1, :]
                l_dec = jnp.exp(jnp.where(causal, col - rowv, -jnp.inf))
                scores = (cb * l_dec).astype(BF16)
                x_m = (jnp.where(m0, x_pair, 0.0) if hh == 0 else jnp.where(m0, 0.0, x_pair)).astype(BF16)
                part = _dot(scores, x_m)
                y_pair = part if y_pair is None else y_pair + part
            y_diag.append(y_pair)
        y = jnp.concatenate(y_diag, axis=1) + y_off + xs_g * dsk_ref[:, gs]
        last_g = acs_last_x[:, gs]
        x_dec = (x_g * jnp.exp(last_g - acs_g)).astype(BF16)
        st_ref[gi] = s_old * jnp.exp(last_g) + _dot_tn(b_g, x_dec)
        z = z_ref[:, gs]
        yg = y * (z * _sigmoid(z))
        ms = jnp.mean(yg * yg, axis=-1, keepdims=True)
        o_ref[:, gs] = (yg * lax.rsqrt(ms + NORM_EPS) * nw_ref[:, gs]).astype(o_ref.dtype)


def _ssd_branch(p_z, p_xbc, small, conv_w, conv_b, dt_bias, a_log, d_skip, norm_w):
    bsz, t, width = p_z.shape
    cw = p_xbc.shape[2]
    heads = width // HEAD
    ch = MB_CHUNK
    pad = lambda v: jnp.zeros((1, LANES), F32).at[0, :heads].set(v.astype(F32))
    hid = jnp.arange(LANES)[:, None]
    col = jnp.arange(width)[None, :]
    expand = (hid == col // HEAD).astype(BF16)
    d_x = jnp.repeat(d_skip.astype(F32), HEAD).reshape(1, width)
    full = lambda shape: pl.BlockSpec(shape, lambda b, c: (0, 0))
    return pl.pallas_call(
        _ssd_kernel,
        grid=(bsz, t // ch),
        in_specs=[pl.BlockSpec((None, ch, cw), lambda b, c: (b, c, 0)),
                  pl.BlockSpec((None, ch, width), lambda b, c: (b, c, 0)),
                  pl.BlockSpec((None, ch, LANES), lambda b, c: (b, c, SM_DT // LANES)),
                  full((MB_CONV, cw)), full((1, cw)), full((1, LANES)), full((1, LANES)),
                  full((1, width)), full((1, width)), full((LANES, width))],
        out_specs=pl.BlockSpec((None, ch, width), lambda b, c: (b, c, 0)),
        out_shape=jax.ShapeDtypeStruct((bsz, t, width), BF16),
        scratch_shapes=[pltpu.VMEM((ch + 8, cw), F32),
                        pltpu.VMEM((ch, cw), F32),
                        pltpu.VMEM((MB_GROUPS, MB_STATE, width // MB_GROUPS), F32)],
        compiler_params=_params(("parallel", "arbitrary")),
        name="mamba2_branch",
    )(p_xbc, p_z, small, conv_w.astype(F32), conv_b.reshape(1, cw).astype(F32), pad(dt_bias), pad(a_log),
      d_x, norm_w.reshape(1, width).astype(F32), expand)


def _pad_cols(w, n):
    return jnp.zeros((w.shape[0], n), w.dtype).at[:, :w.shape[1]].set(w)


def _pad_rows(w, n):
    return jnp.zeros((n, w.shape[1]), w.dtype).at[:w.shape[0]].set(w)


def _layer(x, c_act_mod, l, norm1_g, w_in, rw_mu_rkv, rw_mu_wag, rw_w0, rw_w1, rw_w2, rw_a0, rw_a1, rw_a2, rw_g1,
           rw_g2, rw_k_k, rw_k_a, rw_r_k, rw_lnx_w, rw_lnx_b, mb_conv_w, mb_conv_b, mb_dt_bias, mb_a_log, mb_d,
           mb_norm_w, w_branch_a, w_branch_b, w_gate, b_gate, w_out, norm2_g, mlp_up, mlp_down, out_g, out_affine):
    bsz, t, d = x.shape
    m = bsz * t
    w = rw_k_k.shape[1]
    mbw = mb_norm_w.shape[1]
    cw = mb_conv_b.shape[1]
    off_rkv, off_z, off_xbc = 3 * w, 3 * w + mbw, 3 * w + mbw + cw
    sh1, sc1, gt1, sh2, sc2, gt2 = jnp.split(c_act_mod, 6, axis=-1)

    w_small = jnp.concatenate([_pad_cols(rw_w1[l], LANES), _pad_cols(rw_a1[l], LANES), rw_g1[l],
                               _pad_cols(w_in[l][:, off_xbc:], LANES)], axis=1).astype(BF16)
    h, small = _norm1(x, norm1_g[l], sc1, sh1, rw_mu_wag[l], w_small)
    h2d = h.reshape(m, d)
    wi = w_in[l].astype(BF16)
    p_rkv = _matmul(h2d, wi[:, :off_rkv], F32, name="proj_rkv").reshape(bsz, t, off_rkv)
    p_z = _matmul(h2d, wi[:, off_rkv:off_z], F32, name="proj_z").reshape(bsz, t, mbw)
    p_xbc = _matmul(h2d, wi[:, off_z:off_xbc], F32, name="proj_xbc").reshape(bsz, t, cw)
    gate = _matmul(h2d, w_gate[l].astype(BF16), BF16, bias=b_gate[l], act="sigmoid", name="proj_gate")

    o_a = _rwkv_branch(p_rkv, small, rw_mu_rkv[l], rw_w0[l], _pad_rows(rw_w2[l], LANES).astype(BF16), rw_a0[l],
                       _pad_rows(rw_a2[l], LANES).astype(BF16), rw_g2[l].astype(BF16), rw_k_k[l], rw_k_a[l],
                       rw_r_k[l], rw_lnx_w[l], rw_lnx_b[l])
    o_b = _ssd_branch(p_z, p_xbc, small, mb_conv_w[l], mb_conv_b[l], mb_dt_bias[l], mb_a_log[l], mb_d[l],
                      mb_norm_w[l])
    merged = _merge(o_a.reshape(m, w), o_b.reshape(m, mbw), w_branch_a[l].astype(BF16),
                    w_branch_b[l].astype(BF16), gate)
    x1, h2 = _resid_norm(merged, w_out[l].astype(BF16), x.reshape(m, d), gt1, norm2_g[l], sc2, sh2,
                         rows_per_batch=t, norm_dtype=BF16, emit_x=True, name="out_proj_norm2")
    u = _matmul(h2, mlp_up[l].astype(BF16), BF16, act="relu2", name="mlp_up")
    if out_affine is None:
        (y,) = _resid_norm(u, mlp_down[l].astype(BF16), x1, gt2, out_g, rows_per_batch=t, norm_dtype=F32,
                           emit_x=False, name="mlp_down_final_norm")
        return y.reshape(bsz, t, d)
    raise NotImplementedError("only a single layer followed by the final norm is implemented")


def kernel(x, c, ada_w, ada_b, norm1_g, w_in, rw_mu_rkv, rw_mu_wag, rw_w0, rw_w1, rw_w2, rw_a0, rw_a1, rw_a2, rw_g1, rw_g2, rw_k_k, rw_k_a, rw_r_k, rw_lnx_w, rw_lnx_b, mb_conv_w, mb_conv_b, mb_dt_bias, mb_a_log, mb_d, mb_norm_w, w_branch_a, w_branch_b, w_gate, b_gate, w_out, norm2_g, mlp_up, mlp_down, final_norm_g):
    depth = ada_w.shape[0]
    assert depth == 1, "the fused final-norm epilogue assumes a single layer"
    mod = _ada_mod(c, ada_w[0], ada_b[0])
    return _layer(x, mod, 0, norm1_g, w_in, rw_mu_rkv, rw_mu_wag, rw_w0, rw_w1, rw_w2, rw_a0, rw_a1, rw_a2, rw_g1,
                  rw_g2, rw_k_k, rw_k_a, rw_r_k, rw_lnx_w, rw_lnx_b, mb_conv_w, mb_conv_b, mb_dt_bias, mb_a_log,
                  mb_d, mb_norm_w, w_branch_a, w_branch_b, w_gate, b_gate, w_out, norm2_g, mlp_up, mlp_down,
                  final_norm_g, None)
```

```python
import functools

import jax
import jax.numpy as jnp
from jax import lax
from jax.experimental import pallas as pl
from jax.experimental.pallas import tpu as pltpu

F32 = jnp.float32
BF16 = jnp.bfloat16

LANES = 128
HEAD = 64
RW_CHUNK = 64
RW_DECAY_LORA = 96
RW_AAA_LORA = 96
RW_GATE_LORA = 256
RW_LNX_EPS = 64e-5
MB_GROUPS = 8
MB_STATE = 128
MB_CONV = 4
MB_CHUNK = 128
NORM_EPS = 1e-5
VMEM_LIMIT = 56 * 1024 * 1024

SM_W, SM_A, SM_G, SM_DT, SM_WIDTH = 0, 128, 256, 512, 640


def _params(sem):
    return pltpu.CompilerParams(dimension_semantics=sem, vmem_limit_bytes=VMEM_LIMIT)


def _sigmoid(x):
    return 1.0 / (1.0 + jnp.exp(-x))


def _softplus(x):
    return jnp.maximum(x, 0.0) + jnp.log1p(jnp.exp(-jnp.abs(x)))


def _dot(a, b):
    return jnp.dot(a, b, preferred_element_type=F32)


def _dot_nt(a, b):
    return lax.dot_general(a, b, (((1,), (1,)), ((), ())), preferred_element_type=F32)


def _dot_tn(a, b):
    return lax.dot_general(a, b, (((0,), (0,)), ((), ())), preferred_element_type=F32)


def _split_bf16(x, terms):
    parts = []
    for _ in range(terms - 1):
        hi = x.astype(BF16)
        parts.append(hi)
        x = x - hi.astype(F32)
    parts.append(x.astype(BF16))
    return parts


def _dot_split(dot_fn, x, m, x_is_lhs, terms=3):
    parts = _split_bf16(x, terms)
    prods = [dot_fn(p, m) if x_is_lhs else dot_fn(m, p) for p in parts]
    out = prods[0]
    for p in prods[1:]:
        out = out + p
    return out


def _ada_kernel(c_ref, w_ref, b_ref, o_ref):
    c = c_ref[...]
    ca = (c * _sigmoid(c)).astype(BF16)
    o_ref[...] = _dot(ca, w_ref[...].astype(BF16)) + b_ref[...]


def _ada_mod(c, ada_w, ada_b):
    bsz, d = c.shape
    n = ada_w.shape[1]
    rows = 8
    c_pad = jnp.zeros((rows, d), F32).at[:bsz].set(c)
    tn = 1024
    out = pl.pallas_call(
        _ada_kernel,
        grid=(n // tn,),
        in_specs=[pl.BlockSpec((rows, d), lambda j: (0, 0)),
                  pl.BlockSpec((d, tn), lambda j: (0, j)),
                  pl.BlockSpec((1, tn), lambda j: (0, j))],
        out_specs=pl.BlockSpec((rows, tn), lambda j: (0, j)),
        out_shape=jax.ShapeDtypeStruct((rows, n), F32),
        compiler_params=_params(("parallel",)),
        name="ada_mod",
    )(c_pad, ada_w, ada_b.reshape(1, n))
    return out[:bsz]


def _norm1_kernel(x_ref, g_ref, sc_ref, sh_ref, mu_ref, ws_ref, h_ref, sm_ref, prev_ref):
    i = pl.program_id(1)

    @pl.when(i == 0)
    def _():
        prev_ref[...] = jnp.zeros_like(prev_ref)

    x = x_ref[...]
    tm = x.shape[0]
    ms = jnp.mean(x * x, axis=-1, keepdims=True)
    h = x * lax.rsqrt(ms + NORM_EPS) * g_ref[...] * (1.0 + sc_ref[...]) + sh_ref[...]
    hs = pltpu.roll(h, 1, 0)
    row = lax.broadcasted_iota(jnp.int32, h.shape, 0)
    hs = jnp.where(row == 0, prev_ref[0:1, :], hs)
    prev_ref[0:1, :] = h[tm - 1:tm, :]
    dh = hs - h
    hb = h.astype(BF16)
    h_ref[...] = hb
    xw = (h + dh * mu_ref[0:1, :]).astype(BF16)
    sm_ref[:, SM_W:SM_A] = _dot(xw, ws_ref[:, SM_W:SM_A])
    xa = (h + dh * mu_ref[1:2, :]).astype(BF16)
    sm_ref[:, SM_A:SM_G] = _dot(xa, ws_ref[:, SM_A:SM_G])
    xg = (h + dh * mu_ref[2:3, :]).astype(BF16)
    sm_ref[:, SM_G:SM_DT] = _dot(xg, ws_ref[:, SM_G:SM_DT])
    sm_ref[:, SM_DT:SM_WIDTH] = _dot(hb, ws_ref[:, SM_DT:SM_WIDTH])


def _norm1(x, g, sc, sh, mu_wag, w_small, tm=256):
    bsz, t, d = x.shape
    return pl.pallas_call(
        _norm1_kernel,
        grid=(bsz, t // tm),
        in_specs=[pl.BlockSpec((None, tm, d), lambda b, i: (b, i, 0)),
                  pl.BlockSpec((1, d), lambda b, i: (0, 0)),
                  pl.BlockSpec((None, 1, d), lambda b, i: (b, 0, 0)),
                  pl.BlockSpec((None, 1, d), lambda b, i: (b, 0, 0)),
                  pl.BlockSpec((3, d), lambda b, i: (0, 0)),
                  pl.BlockSpec((d, SM_WIDTH), lambda b, i: (0, 0))],
        out_specs=[pl.BlockSpec((None, tm, d), lambda b, i: (b, i, 0)),
                   pl.BlockSpec((None, tm, SM_WIDTH), lambda b, i: (b, i, 0))],
        out_shape=[jax.ShapeDtypeStruct((bsz, t, d), BF16),
                   jax.ShapeDtypeStruct((bsz, t, SM_WIDTH), F32)],
        scratch_shapes=[pltpu.VMEM((8, d), F32)],
        compiler_params=_params(("parallel", "arbitrary")),
        name="norm1_small_proj",
    )(x, g.reshape(1, d), sc.reshape(bsz, 1, d), sh.reshape(bsz, 1, d), mu_wag, w_small)


def _mm_kernel(a_ref, w_ref, *rest, act, has_bias):
    if has_bias:
        bias_ref, o_ref, wb_ref = rest
    else:
        o_ref, wb_ref = rest

    @pl.when(pl.program_id(1) == 0)
    def _():
        wb_ref[...] = w_ref[...].astype(BF16)

    y = _dot(a_ref[...], wb_ref[...])
    if has_bias:
        y = y + bias_ref[...]
    if act == "sigmoid":
        y = _sigmoid(y)
    elif act == "relu2":
        y = jnp.maximum(y, 0.0)
        y = y * y
    o_ref[...] = y.astype(o_ref.dtype)


def _matmul(a, w, out_dtype, col0=0, n=None, bias=None, act=None, tm=1024, tn=1024, name="matmul"):
    m, k = a.shape
    n = w.shape[1] if n is None else n
    tm, tn = min(tm, m), min(tn, n)
    assert col0 % tn == 0 and n % tn == 0 and m % tm == 0
    jb = col0 // tn
    in_specs = [pl.BlockSpec((tm, k), lambda j, i: (i, 0)),
                pl.BlockSpec((k, tn), lambda j, i: (0, j + jb))]
    args = [a, w]
    if bias is not None:
        in_specs.append(pl.BlockSpec((1, tn), lambda j, i: (0, j)))
        args.append(bias.reshape(1, n).astype(F32))
    return pl.pallas_call(
        functools.partial(_mm_kernel, act=act, has_bias=bias is not None),
        grid=(n // tn, m // tm),
        in_specs=in_specs,
        out_specs=pl.BlockSpec((tm, tn), lambda j, i: (i, j)),
        out_shape=jax.ShapeDtypeStruct((m, n), out_dtype),
        scratch_shapes=[pltpu.VMEM((k, tn), BF16)],
        compiler_params=_params(("parallel", "arbitrary")),
        name=name,
    )(*args)


def _merge_kernel(oa_ref, ob_ref, wa_ref, wb_ref, ga_ref, gb_ref, o_ref, wab_ref, wbb_ref):
    @pl.when(pl.program_id(1) == 0)
    def _():
        wab_ref[...] = wa_ref[...].astype(BF16)
        wbb_ref[...] = wb_ref[...].astype(BF16)

    ya = _dot(oa_ref[...], wab_ref[...])
    yb = _dot(ob_ref[...], wbb_ref[...])
    o_ref[...] = (ga_ref[...].astype(F32) * ya + gb_ref[...].astype(F32) * yb).astype(o_ref.dtype)


def _merge(o_a, o_b, wa, wb, gate, tm=512, tn=512):
    m, ka = o_a.shape
    kb = o_b.shape[1]
    n = wa.shape[1]
    off = n // tn
    return pl.pallas_call(
        _merge_kernel,
        grid=(n // tn, m // tm),
        in_specs=[pl.BlockSpec((tm, ka), lambda j, i: (i, 0)),
                  pl.BlockSpec((tm, kb), lambda j, i: (i, 0)),
                  pl.BlockSpec((ka, tn), lambda j, i: (0, j)),
                  pl.BlockSpec((kb, tn), lambda j, i: (0, j)),
                  pl.BlockSpec((tm, tn), lambda j, i: (i, j)),
                  pl.BlockSpec((tm, tn), lambda j, i: (i, j + off))],
        out_specs=pl.BlockSpec((tm, tn), lambda j, i: (i, j)),
        out_shape=jax.ShapeDtypeStruct((m, n), BF16),
        scratch_shapes=[pltpu.VMEM((ka, tn), BF16), pltpu.VMEM((kb, tn), BF16)],
        compiler_params=_params(("parallel", "arbitrary")),
        name="branch_merge",
    )(o_a, o_b, wa, wb, gate, gate)


def _resid_norm_kernel(a_ref, b_ref, x_ref, gt_ref, g_ref, *rest, affine, emit_x):
    rest = list(rest)
    if affine:
        sc_ref, sh_ref = rest[0], rest[1]
        rest = rest[2:]
    if emit_x:
        xo_ref, no_ref, acc_ref = rest
    else:
        no_ref, acc_ref = rest
    kk = pl.program_id(1)

    @pl.when(kk == 0)
    def _():
        acc_ref[...] = jnp.zeros_like(acc_ref)

    acc_ref[...] += _dot(a_ref[...], b_ref[...])

    @pl.when(kk == pl.num_programs(1) - 1)
    def _():
        xn = x_ref[...] + gt_ref[...] * acc_ref[...]
        if emit_x:
            xo_ref[...] = xn
        ms = jnp.mean(xn * xn, axis=-1, keepdims=True)
        y = xn * lax.rsqrt(ms + NORM_EPS) * g_ref[...]
        if affine:
            y = y * (1.0 + sc_ref[...]) + sh_ref[...]
        no_ref[...] = y.astype(no_ref.dtype)


def _resid_norm(a, b, x, gt, g, sc=None, sh=None, *, rows_per_batch, norm_dtype, emit_x, tm=512, tk=2048,
                name="resid_norm"):
    m, k = a.shape
    n = b.shape[1]
    bsz = gt.shape[0]
    tpb = rows_per_batch // tm
    affine = sc is not None
    vec = lambda v: v.reshape(bsz, 1, n)
    in_specs = [pl.BlockSpec((tm, tk), lambda i, kk: (i, kk)),
                pl.BlockSpec((tk, n), lambda i, kk: (kk, 0)),
                pl.BlockSpec((tm, n), lambda i, kk: (i, 0)),
                pl.BlockSpec((None, 1, n), lambda i, kk: (i // tpb, 0, 0)),
                pl.BlockSpec((1, n), lambda i, kk: (0, 0))]
    args = [a, b, x, vec(gt), g.reshape(1, n)]
    if affine:
        in_specs += [pl.BlockSpec((None, 1, n), lambda i, kk: (i // tpb, 0, 0))] * 2
        args += [vec(sc), vec(sh)]
    out_specs = [pl.BlockSpec((tm, n), lambda i, kk: (i, 0))]
    out_shape = [jax.ShapeDtypeStruct((m, n), norm_dtype)]
    if emit_x:
        out_specs = [pl.BlockSpec((tm, n), lambda i, kk: (i, 0))] + out_specs
        out_shape = [jax.ShapeDtypeStruct((m, n), F32)] + out_shape
    return pl.pallas_call(
        functools.partial(_resid_norm_kernel, affine=affine, emit_x=emit_x),
        grid=(m // tm, k // tk),
        in_specs=in_specs,
        out_specs=out_specs,
        out_shape=out_shape,
        scratch_shapes=[pltpu.VMEM((tm, n), F32)],
        compiler_params=_params(("parallel", "arbitrary")),
        name=name,
    )(*args)


def _seg_sum(x, ones_bd):
    cols = x.shape[1]
    wid = ones_bd.shape[0]
    outs = [_dot_split(_dot, x[:, c:c + wid], ones_bd, True, terms=2) for c in range(0, cols, wid)]
    return outs[0] if len(outs) == 1 else jnp.concatenate(outs, axis=1)


def _inv_unit_lower_all(a_list, ri, ci):
    n = a_list[0].shape[0]
    eye = (ri == ci).astype(F32)
    same16 = (ri // 16) == (ci // 16)
    same32 = (ri // 32) == (ci // 32)
    t = [eye + jnp.where(same16, a, 0.0) for a in a_list]
    p = [jnp.where(same16, a, 0.0).astype(BF16) for a in a_list]
    p = [_dot(x, x).astype(BF16) for x in p]
    for _ in range(2):
        r = [_dot(x, jnp.concatenate([tt.astype(BF16), x], axis=1)) for x, tt in zip(p, t)]
        t = [tt + rr[:, :n] for tt, rr in zip(t, r)]
        p = [rr[:, n:].astype(BF16) for rr in r]
    t = [tt + _dot(x, tt.astype(BF16)) for x, tt in zip(p, t)]
    for half, sel in ((16, same32 & (~same16)), (32, ~same32)):
        lo = [s for s in range(0, n, half) if (s // half) % 2 == 1]
        tb = [tt.astype(BF16) for tt in t]
        t_lo = [jnp.concatenate([tt[s:s + half] for s in lo], axis=0).astype(BF16) for tt in t]
        x = [_dot(tl, jnp.where(sel, a, 0.0).astype(BF16)).astype(BF16) for tl, a in zip(t_lo, a_list)]
        upd = [_dot(xx, b) for xx, b in zip(x, tb)]

        def add_rows(tt, u):
            parts = [tt[s:s + half] for s in range(0, n, half)]
            for j, s in enumerate(lo):
                parts[s // half] = parts[s // half] + u[j * half:(j + 1) * half]
            return jnp.concatenate(parts, axis=0)

        t = [add_rows(tt, u) for tt, u in zip(t, upd)]
    return t


def _rwkv_kernel(pr_ref, pk_ref, pv_ref, sm_ref, mur_ref, muk_ref, muv_ref, w0_ref, w2_ref, a0_ref, a2_ref,
                 g2_ref, kk_ref, ka_ref, rk_ref, lnw_ref, lnb_ref, o_ref, s_ref, prev_ref, y_ref):
    tb, hw = pr_ref.shape
    n_chunk = tb // RW_CHUNK
    n_pair = hw // LANES
    c_len = RW_CHUNK

    @pl.when(pl.program_id(2) == 0)
    def _():
        s_ref[...] = jnp.zeros_like(s_ref)
        prev_ref[...] = jnp.zeros_like(prev_ref)

    row = lax.broadcasted_iota(jnp.int32, (tb, hw), 0)

    def shifted_mix(p_ref, mu_ref, slot):
        p = p_ref[...]
        ps = pltpu.roll(p, 1, 0)
        ps = jnp.where(row == 0, prev_ref[slot:slot + 1, :], ps)
        prev_ref[slot:slot + 1, :] = p[tb - 1:tb, :]
        return p + (ps - p) * mu_ref[...]

    r = shifted_mix(pr_ref, mur_ref, 0)
    k = shifted_mix(pk_ref, muk_ref, 1)
    v = shifted_mix(pv_ref, muv_ref, 2)

    lw = jnp.tanh(sm_ref[:, SM_W:SM_A]).astype(BF16)
    w_log = -_softplus(-(w0_ref[...] + _dot(lw, w2_ref[...]))) - 0.5
    logw = -jnp.exp(w_log)
    a = _sigmoid(a0_ref[...] + _dot(sm_ref[:, SM_A:SM_G].astype(BF16), a2_ref[...]))
    g = _dot(_sigmoid(sm_ref[:, SM_G:SM_DT]).astype(BF16), g2_ref[...])

    li = lax.broadcasted_iota(jnp.int32, (LANES, LANES), 0)
    lj = lax.broadcasted_iota(jnp.int32, (LANES, LANES), 1)
    same_head = (li // HEAD) == (lj // HEAD)
    seg_w = min(2 * LANES, hw)
    si = lax.broadcasted_iota(jnp.int32, (seg_w, seg_w), 0)
    sj = lax.broadcasted_iota(jnp.int32, (seg_w, seg_w), 1)
    ones_bd = ((si // HEAD) == (sj // HEAD)).astype(BF16)

    kkv = k * kk_ref[...]
    kkn = kkv / jnp.maximum(jnp.sqrt(_seg_sum(kkv * kkv, ones_bd)), 1e-12)
    k2 = k * (1.0 + (a - 1.0) * ka_ref[...])
    aa = -kkn
    bb = kkn * a
    bonus = _seg_sum(r * k2 * rk_ref[...], ones_bd) * v

    ti = lax.broadcasted_iota(jnp.int32, (tb, tb), 0)
    tj = lax.broadcasted_iota(jnp.int32, (tb, tb), 1)
    cum_mat = (((ti // c_len) == (tj // c_len)) & (tj <= ti)).astype(BF16)
    lc = _dot_split(_dot, logw, cum_mat, False)
    e_pos = jnp.exp(lc)
    e_neg = jnp.exp(-lc)
    rt = r * e_pos
    at = aa * jnp.exp(lc - logw)
    bt = bb * e_neg
    kt = k2 * e_neg

    lane = lax.broadcasted_iota(jnp.int32, (c_len, LANES), 1)
    m0 = lane < HEAD

    def bd(x):
        return jnp.concatenate([jnp.where(m0, x, 0.0), jnp.where(m0, 0.0, x)], axis=0).astype(BF16)

    strict = same_head & ((li % HEAD) > (lj % HEAD))
    incl = same_head & ((li % HEAD) >= (lj % HEAD))

    inst = [(c, p) for c in range(n_chunk) for p in range(n_pair)]
    rows = lambda c: slice(c * c_len, (c + 1) * c_len)
    lanes = lambda p: slice(p * LANES, (p + 1) * LANES)
    tile = lambda x: [bd(x[rows(c), lanes(p)]) for c, p in inst]
    a_bd, r_bd, b_bd, k_bd, v_bd = tile(at), tile(rt), tile(bt), tile(kt), tile(v)

    bk_bd = [jnp.concatenate([z, w], axis=0) for z, w in zip(b_bd, k_bd)]
    gm = [_dot_nt(jnp.concatenate([x, y], axis=0), zw) for x, y, zw in zip(a_bd, r_bd, bk_bd)]
    a_ab = [jnp.where(strict, x[0:LANES, 0:LANES], 0.0) for x in gm]
    a_ak = [jnp.where(strict, x[0:LANES, LANES:], 0.0).astype(BF16) for x in gm]
    incl2 = jnp.concatenate([incl, incl], axis=1)
    a_rbk = [jnp.where(incl2, x[LANES:, :], 0.0).astype(BF16) for x in gm]
    t_inv = [x.astype(BF16) for x in _inv_unit_lower_all(a_ab, li, lj)]
    av = [_dot(x, y).astype(BF16) for x, y in zip(a_ak, v_bd)]
    tav = [_dot(t, jnp.concatenate([x, y], axis=1)) for t, x, y in zip(t_inv, a_bd, av)]
    ta = [x[:, :LANES].astype(BF16) for x in tav]
    uv = [x[:, LANES:] for x in tav]

    state = [s_ref[p] for p in range(n_pair)]
    for c in range(n_chunk):
        ids = [c * n_pair + p for p in range(n_pair)]
        s_b = [s.astype(BF16) for s in state]
        u_bd = [(_dot_nt(ta[i], s_b[p]) + uv[i]).astype(BF16) for p, i in enumerate(ids)]
        g_end = [e_pos[(c + 1) * c_len - 1:(c + 1) * c_len, lanes(p)] for p in range(n_pair)]
        uv_bd = [jnp.concatenate([u_bd[p], v_bd[i]], axis=0) for p, i in enumerate(ids)]
        state = [g_end[p] * (state[p] + _dot_tn(uv_bd[p], bk_bd[i])) for p, i in enumerate(ids)]
        for p, i in enumerate(ids):
            y_bd = _dot_nt(r_bd[i], s_b[p]) + _dot(a_rbk[i], uv_bd[p])
            y_ref[rows(c), lanes(p)] = y_bd[0:c_len] + y_bd[c_len:]
    for p in range(n_pair):
        s_ref[p] = state[p]

    y = y_ref[...]
    mu = _seg_sum(y, ones_bd) * (1.0 / HEAD)
    d = y - mu
    var = _seg_sum(d * d, ones_bd) * (1.0 / HEAD)
    yn = d * lax.rsqrt(var + RW_LNX_EPS) * lnw_ref[...] + lnb_ref[...]
    o_ref[...] = ((yn + bonus) * g).astype(o_ref.dtype)


def _rwkv_branch(p_rkv, small, mu_rkv, w0, w2, a0, a2, g2, k_k, k_a, r_k, lnx_w, lnx_b, tb=256, hw=1024):
    bsz, t, w3 = p_rkv.shape
    w = w3 // 3
    nb = w // hw
    row = lambda v: v.reshape(1, -1).astype(F32)
    vec_spec = lambda off: pl.BlockSpec((1, hw), lambda b, gi, ti: (0, gi + off))
    rkv_spec = lambda off: pl.BlockSpec((None, tb, hw), lambda b, gi, ti: (b, ti, gi + off))
    mat_spec = lambda rows: pl.BlockSpec((rows, hw), lambda b, gi, ti: (0, gi))
    mu = row(mu_rkv)
    return pl.pallas_call(
        _rwkv_kernel,
        grid=(bsz, nb, t // tb),
        in_specs=[rkv_spec(0), rkv_spec(nb), rkv_spec(2 * nb),
                  pl.BlockSpec((None, tb, SM_WIDTH), lambda b, gi, ti: (b, ti, 0)),
                  vec_spec(0), vec_spec(nb), vec_spec(2 * nb),
                  vec_spec(0), mat_spec(LANES), vec_spec(0), mat_spec(LANES), mat_spec(RW_GATE_LORA),
                  vec_spec(0), vec_spec(0), vec_spec(0), vec_spec(0), vec_spec(0)],
        out_specs=pl.BlockSpec((None, tb, hw), lambda b, gi, ti: (b, ti, gi)),
        out_shape=jax.ShapeDtypeStruct((bsz, t, w), BF16),
        scratch_shapes=[pltpu.VMEM((hw // LANES, LANES, LANES), F32),
                        pltpu.VMEM((8, hw), F32),
                        pltpu.VMEM((tb, hw), F32)],
        compiler_params=_params(("parallel", "parallel", "arbitrary")),
        name="rwkv7_branch",
    )(p_rkv, p_rkv, p_rkv, small, mu, mu, mu, row(w0), w2, row(a0), a2, g2,
      row(k_k), row(k_a), row(r_k), row(lnx_w), row(lnx_b))


def _ssd_kernel(xbc_ref, z_ref, dt_ref, cw_ref, cb_ref, dtb_ref, alog_ref, dsk_ref, nw_ref, exp_ref,
                o_ref, ext_ref, act_ref, st_ref):
    ch, cw = xbc_ref.shape
    width = z_ref.shape[1]
    gw = width // MB_GROUPS
    tail = 8

    @pl.when(pl.program_id(1) == 0)
    def _():
        st_ref[...] = jnp.zeros_like(st_ref)
        ext_ref[0:tail, :] = jnp.zeros((tail, cw), F32)

    @pl.when(pl.program_id(1) != 0)
    def _():
        ext_ref[0:tail, :] = ext_ref[ch:ch + tail, :]

    ext_ref[tail:tail + ch, :] = xbc_ref[...]

    cblk = 512
    for c0 in range(0, cw, cblk):
        e = ext_ref[:, c0:c0 + cblk]
        acc = cb_ref[:, c0:c0 + cblk] + cw_ref[MB_CONV - 1:MB_CONV, c0:c0 + cblk] * e[tail:]
        for s in range(1, MB_CONV):
            acc = acc + cw_ref[MB_CONV - 1 - s:MB_CONV - s, c0:c0 + cblk] * pltpu.roll(e, s, 0)[tail:]
        act_ref[:, c0:c0 + cblk] = acc * _sigmoid(acc)

    dt = _softplus(dt_ref[...] + dtb_ref[...])
    d_a = dt * (-jnp.exp(alog_ref[...]))
    ti = lax.broadcasted_iota(jnp.int32, (ch, ch), 0)
    tj = lax.broadcasted_iota(jnp.int32, (ch, ch), 1)
    causal = tj <= ti
    acs = _dot_split(_dot, d_a, causal.astype(BF16), False)
    acs_t = _dot_split(_dot_tn, d_a, (ti <= tj).astype(BF16), True)
    expand = exp_ref[...]
    dt_x = _dot_split(_dot, dt, expand, True)
    acs_x = _dot_split(_dot, acs, expand, True)
    acs_last_x = acs_x[ch - 1:ch, :]
    lane = lax.broadcasted_iota(jnp.int32, (ch, LANES), 1)
    m0 = lane < HEAD

    for gi in range(MB_GROUPS):
        gs = slice(gi * gw, (gi + 1) * gw)
        c_g = act_ref[:, width + MB_GROUPS * MB_STATE + gi * MB_STATE:width + MB_GROUPS * MB_STATE + (gi + 1) * MB_STATE].astype(BF16)
        b_g = act_ref[:, width + gi * MB_STATE:width + (gi + 1) * MB_STATE].astype(BF16)
        xs_g = act_ref[:, gs]
        acs_g = acs_x[:, gs]
        x_g = xs_g * dt_x[:, gs]
        cb = _dot_nt(c_g, b_g)
        s_old = st_ref[gi]
        y_off = _dot(c_g, s_old.astype(BF16)) * jnp.exp(acs_g)
        y_diag = []
        for q in range(gw // LANES):
            x_pair = x_g[:, q * LANES:(q + 1) * LANES]
            y_pair = None
            for hh in range(2):
                head = (gi * gw) // HEAD + 2 * q + hh
                col = jnp.broadcast_to(acs[:, head:head + 1], (ch, ch))
                rowv = acs_t[head:head + 1, :]
                l_dec = jnp.exp(jnp.where(causal, col - rowv, -jnp.inf))
                scores = (cb * l_dec).astype(BF16)
                x_m = (jnp.where(m0, x_pair, 0.0) if hh == 0 else jnp.where(m0, 0.0, x_pair)).astype(BF16)
                part = _dot(scores, x_m)
                y_pair = part if y_pair is None else y_pair + part
            y_diag.append(y_pair)
        y = jnp.concatenate(y_diag, axis=1) + y_off + xs_g * dsk_ref[:, gs]
        last_g = acs_last_x[:, gs]
        x_dec = (x_g * jnp.exp(last_g - acs_g)).astype(BF16)
        st_ref[gi] = s_old * jnp.exp(last_g) + _dot_tn(b_g, x_dec)
        z = z_ref[:, gs]
        yg = y * (z * _sigmoid(z))
        ms = jnp.mean(yg * yg, axis=-1, keepdims=True)
        o_ref[:, gs] = (yg * lax.rsqrt(ms + NORM_EPS) * nw_ref[:, gs]).astype(o_ref.dtype)


def _ssd_branch(p_z, p_xbc, small, conv_w, conv_b, dt_bias, a_log, d_skip, norm_w):
    bsz, t, width = p_z.shape
    cw = p_xbc.shape[2]
    heads = width // HEAD
    ch = MB_CHUNK
    pad = lambda v: jnp.zeros((1, LANES), F32).at[0, :heads].set(v.astype(F32))
    hid = jnp.arange(LANES)[:, None]
    col = jnp.arange(width)[None, :]
    expand = (hid == col // HEAD).astype(BF16)
    d_x = jnp.repeat(d_skip.astype(F32), HEAD).reshape(1, width)
    full = lambda shape: pl.BlockSpec(shape, lambda b, c: (0, 0))
    return pl.pallas_call(
        _ssd_kernel,
        grid=(bsz, t // ch),
        in_specs=[pl.BlockSpec((None, ch, cw), lambda b, c: (b, c, 0)),
                  pl.BlockSpec((None, ch, width), lambda b, c: (b, c, 0)),
                  pl.BlockSpec((None, ch, LANES), lambda b, c: (b, c, SM_DT // LANES)),
                  full((MB_CONV, cw)), full((1, cw)), full((1, LANES)), full((1, LANES)),
                  full((1, width)), full((1, width)), full((LANES, width))],
        out_specs=pl.BlockSpec((None, ch, width), lambda b, c: (b, c, 0)),
        out_shape=jax.ShapeDtypeStruct((bsz, t, width), BF16),
        scratch_shapes=[pltpu.VMEM((ch + 8, cw), F32),
                        pltpu.VMEM((ch, cw), F32),
                        pltpu.VMEM((MB_GROUPS, MB_STATE, width // MB_GROUPS), F32)],
        compiler_params=_params(("parallel", "arbitrary")),
        name="mamba2_branch",
    )(p_xbc, p_z, small, conv_w.astype(F32), conv_b.reshape(1, cw).astype(F32), pad(dt_bias), pad(a_log),
      d_x, norm_w.reshape(1, width).astype(F32), expand)


def _pad_cols(w, n):
    return jnp.zeros((w.shape[0], n), w.dtype).at[:, :w.shape[1]].set(w)


def _pad_rows(w, n):
    return jnp.zeros((n, w.shape[1]), w.dtype).at[:w.shape[0]].set(w)


def _layer(x, c_act_mod, l, norm1_g, w_in, rw_mu_rkv, rw_mu_wag, rw_w0, rw_w1, rw_w2, rw_a0, rw_a1, rw_a2, rw_g1,
           rw_g2, rw_k_k, rw_k_a, rw_r_k, rw_lnx_w, rw_lnx_b, mb_conv_w, mb_conv_b, mb_dt_bias, mb_a_log, mb_d,
           mb_norm_w, w_branch_a, w_branch_b, w_gate, b_gate, w_out, norm2_g, mlp_up, mlp_down, out_g, out_affine):
    bsz, t, d = x.shape
    m = bsz * t
    w = rw_k_k.shape[1]
    mbw = mb_norm_w.shape[1]
    cw = mb_conv_b.shape[1]
    off_rkv, off_z, off_xbc = 3 * w, 3 * w + mbw, 3 * w + mbw + cw
    sh1, sc1, gt1, sh2, sc2, gt2 = jnp.split(c_act_mod, 6, axis=-1)

    w_small = jnp.concatenate([_pad_cols(rw_w1[l], LANES), _pad_cols(rw_a1[l], LANES), rw_g1[l],
                               _pad_cols(w_in[l][:, off_xbc:], LANES)], axis=1).astype(BF16)
    h, small = _norm1(x, norm1_g[l], sc1, sh1, rw_mu_wag[l], w_small)
    h2d = h.reshape(m, d)
    p_rkv = _matmul(h2d, w_in[l], F32, 0, off_rkv, name="proj_rkv").reshape(bsz, t, off_rkv)
    p_z = _matmul(h2d, w_in[l], F32, off_rkv, mbw, name="proj_z").reshape(bsz, t, mbw)
    p_xbc = _matmul(h2d, w_in[l], F32, off_z, cw, name="proj_xbc").reshape(bsz, t, cw)
    gate = _matmul(h2d, w_gate[l], BF16, bias=b_gate[l], act="sigmoid", name="proj_gate")

    o_a = _rwkv_branch(p_rkv, small, rw_mu_rkv[l], rw_w0[l], _pad_rows(rw_w2[l], LANES).astype(BF16), rw_a0[l],
                       _pad_rows(rw_a2[l], LANES).astype(BF16), rw_g2[l].astype(BF16), rw_k_k[l], rw_k_a[l],
                       rw_r_k[l], rw_lnx_w[l], rw_lnx_b[l])
    o_b = _ssd_branch(p_z, p_xbc, small, mb_conv_w[l], mb_conv_b[l], mb_dt_bias[l], mb_a_log[l], mb_d[l],
                      mb_norm_w[l])
    merged = _merge(o_a.reshape(m, w), o_b.reshape(m, mbw), w_branch_a[l], w_branch_b[l], gate)
    x1, h2 = _resid_norm(merged, w_out[l].astype(BF16), x.reshape(m, d), gt1, norm2_g[l], sc2, sh2,
                         rows_per_batch=t, norm_dtype=BF16, emit_x=True, name="out_proj_norm2")
    u = _matmul(h2, mlp_up[l], BF16, act="relu2", name="mlp_up")
    if out_affine is None:
        (y,) = _resid_norm(u, mlp_down[l].astype(BF16), x1, gt2, out_g, rows_per_batch=t, norm_dtype=F32,
                           emit_x=False, name="mlp_down_final_norm")
        return y.reshape(bsz, t, d)
    raise NotImplementedError("only a single layer followed by the final norm is implemented")


def kernel(x, c, ada_w, ada_b, norm1_g, w_in, rw_mu_rkv, rw_mu_wag, rw_w0, rw_w1, rw_w2, rw_a0, rw_a1, rw_a2, rw_g1, rw_g2, rw_k_k, rw_k_a, rw_r_k, rw_lnx_w, rw_lnx_b, mb_conv_w, mb_conv_b, mb_dt_bias, mb_a_log, mb_d, mb_norm_w, w_branch_a, w_branch_b, w_gate, b_gate, w_out, norm2_g, mlp_up, mlp_down, final_norm_g):
    depth = ada_w.shape[0]
    assert depth == 1, "the fused final-norm epilogue assumes a single layer"
    mod = _ada_mod(c, ada_w[0], ada_b[0])
    return _layer(x, mod, 0, norm1_g, w_in, rw_mu_rkv, rw_mu_wag, rw_w0, rw_w1, rw_w2, rw_a0, rw_a1, rw_a2, rw_g1,
                  rw_g2, rw_k_k, rw_k_a, rw_r_k, rw_lnx_w, rw_lnx_b, mb_conv_w, mb_conv_b, mb_dt_bias, mb_a_log,
                  mb_d, mb_norm_w, w_branch_a, w_branch_b, w_gate, b_gate, w_out, norm2_g, mlp_up, mlp_down,
                  final_norm_g, None)
```

```python
import functools
import math

import jax
import jax.numpy as jnp
from jax import lax
from jax.experimental import pallas as pl
from jax.experimental.pallas import tpu as pltpu

F32 = jnp.float32
BF16 = jnp.bfloat16

LANES = 128
HEAD = 64
RW_CHUNK = 64
RW_DECAY_LORA = 96
RW_AAA_LORA = 96
RW_GATE_LORA = 256
RW_LNX_EPS = 64e-5
MB_GROUPS = 8
MB_STATE = 128
MB_CONV = 4
MB_CHUNK = 128
NORM_EPS = 1e-5
LOG2E = 1.4426950408889634
VMEM_LIMIT = 56 * 1024 * 1024

SM_W, SM_A, SM_G, SM_DT, SM_WIDTH = 0, 128, 256, 512, 640


def _params(sem):
    return pltpu.CompilerParams(dimension_semantics=sem, vmem_limit_bytes=VMEM_LIMIT)


def _sigmoid(x):
    return 1.0 / (1.0 + jnp.exp(-x))


def _softplus(x):
    return jnp.maximum(x, 0.0) + jnp.log1p(jnp.exp(-jnp.abs(x)))


def _dot(a, b):
    return jnp.dot(a, b, preferred_element_type=F32)


def _dot_nt(a, b):
    return lax.dot_general(a, b, (((1,), (1,)), ((), ())), preferred_element_type=F32)


def _dot_tn(a, b):
    return lax.dot_general(a, b, (((0,), (0,)), ((), ())), preferred_element_type=F32)


def _split_bf16(x, terms):
    parts = []
    for _ in range(terms - 1):
        hi = x.astype(BF16)
        parts.append(hi)
        x = x - hi.astype(F32)
    parts.append(x.astype(BF16))
    return parts


def _dot_split(dot_fn, x, m, x_is_lhs, terms=3):
    parts = _split_bf16(x, terms)
    prods = [dot_fn(p, m) if x_is_lhs else dot_fn(m, p) for p in parts]
    out = prods[0]
    for p in prods[1:]:
        out = out + p
    return out


def _ada_kernel(c_ref, w_ref, b_ref, o_ref):
    c = c_ref[...]
    ca = (c * _sigmoid(c)).astype(BF16)
    o_ref[...] = _dot(ca, w_ref[...].astype(BF16)) + b_ref[...]


def _ada_mod(c, ada_w, ada_b):
    bsz, d = c.shape
    n = ada_w.shape[1]
    rows = 8
    c_pad = jnp.zeros((rows, d), F32).at[:bsz].set(c)
    tn = 1024
    out = pl.pallas_call(
        _ada_kernel,
        grid=(n // tn,),
        in_specs=[pl.BlockSpec((rows, d), lambda j: (0, 0)),
                  pl.BlockSpec((d, tn), lambda j: (0, j)),
                  pl.BlockSpec((1, tn), lambda j: (0, j))],
        out_specs=pl.BlockSpec((rows, tn), lambda j: (0, j)),
        out_shape=jax.ShapeDtypeStruct((rows, n), F32),
        compiler_params=_params(("parallel",)),
        name="ada_mod",
    )(c_pad, ada_w, ada_b.reshape(1, n))
    return out[:bsz]


def _norm1_kernel(x_ref, g_ref, sc_ref, sh_ref, mu_ref, ws_ref, h_ref, sm_ref, prev_ref):
    i = pl.program_id(1)

    @pl.when(i == 0)
    def _():
        prev_ref[...] = jnp.zeros_like(prev_ref)

    x = x_ref[...]
    tm = x.shape[0]
    ms = jnp.mean(x * x, axis=-1, keepdims=True)
    h = x * lax.rsqrt(ms + NORM_EPS) * g_ref[...] * (1.0 + sc_ref[...]) + sh_ref[...]
    hs = pltpu.roll(h, 1, 0)
    row = lax.broadcasted_iota(jnp.int32, (8, h.shape[1]), 0)
    hs = jnp.concatenate([jnp.where(row == 0, prev_ref[0:1, :], hs[0:8]), hs[8:]], axis=0)
    prev_ref[0:1, :] = h[tm - 1:tm, :]
    dh = hs - h
    hb = h.astype(BF16)
    h_ref[...] = hb
    xw = (h + dh * mu_ref[0:1, :]).astype(BF16)
    sm_ref[:, SM_W:SM_A] = _dot(xw, ws_ref[:, SM_W:SM_A])
    xa = (h + dh * mu_ref[1:2, :]).astype(BF16)
    sm_ref[:, SM_A:SM_G] = _dot(xa, ws_ref[:, SM_A:SM_G])
    xg = (h + dh * mu_ref[2:3, :]).astype(BF16)
    sm_ref[:, SM_G:SM_DT] = _dot(xg, ws_ref[:, SM_G:SM_DT])
    sm_ref[:, SM_DT:SM_WIDTH] = _dot(hb, ws_ref[:, SM_DT:SM_WIDTH])


def _norm1(x, g, sc, sh, mu_wag, w_small, tm=256):
    bsz, t, d = x.shape
    return pl.pallas_call(
        _norm1_kernel,
        grid=(bsz, t // tm),
        in_specs=[pl.BlockSpec((None, tm, d), lambda b, i: (b, i, 0)),
                  pl.BlockSpec((1, d), lambda b, i: (0, 0)),
                  pl.BlockSpec((None, 1, d), lambda b, i: (b, 0, 0)),
                  pl.BlockSpec((None, 1, d), lambda b, i: (b, 0, 0)),
                  pl.BlockSpec((3, d), lambda b, i: (0, 0)),
                  pl.BlockSpec((d, SM_WIDTH), lambda b, i: (0, 0))],
        out_specs=[pl.BlockSpec((None, tm, d), lambda b, i: (b, i, 0)),
                   pl.BlockSpec((None, tm, SM_WIDTH), lambda b, i: (b, i, 0))],
        out_shape=[jax.ShapeDtypeStruct((bsz, t, d), BF16),
                   jax.ShapeDtypeStruct((bsz, t, SM_WIDTH), F32)],
        scratch_shapes=[pltpu.VMEM((8, d), F32)],
        compiler_params=_params(("parallel", "arbitrary")),
        name="norm1_small_proj",
    )(x, g.reshape(1, d), sc.reshape(bsz, 1, d), sh.reshape(bsz, 1, d), mu_wag, w_small)


def _mm_kernel(a_ref, w_ref, *rest, act, has_bias, w_transposed):
    if has_bias:
        bias_ref, o_ref, wb_ref = rest
    else:
        o_ref, wb_ref = rest

    @pl.when(pl.program_id(1) == 0)
    def _():
        w = w_ref[...]
        wb_ref[...] = (w.T if w_transposed else w).astype(BF16)

    y = _dot(a_ref[...], wb_ref[...])
    if has_bias:
        y = y + bias_ref[...]
    if act == "sigmoid":
        y = _sigmoid(y)
    elif act == "relu2":
        y = jnp.maximum(y, 0.0)
        y = y * y
    o_ref[...] = y.astype(o_ref.dtype)


def _matmul(a, w, out_dtype, col0=0, n=None, bias=None, act=None, w_transposed=False, tm=1024, tn=1024,
            name="matmul"):
    m, k = a.shape
    n = w.shape[0 if w_transposed else 1] if n is None else n
    tm, tn = min(tm, m), min(tn, n)
    assert col0 % tn == 0 and n % tn == 0 and m % tm == 0
    jb = col0 // tn
    w_spec = (pl.BlockSpec((tn, k), lambda j, i: (j + jb, 0)) if w_transposed
              else pl.BlockSpec((k, tn), lambda j, i: (0, j + jb)))
    in_specs = [pl.BlockSpec((tm, k), lambda j, i: (i, 0)), w_spec]
    args = [a, w]
    if bias is not None:
        in_specs.append(pl.BlockSpec((1, tn), lambda j, i: (0, j)))
        args.append(bias.reshape(1, n).astype(F32))
    return pl.pallas_call(
        functools.partial(_mm_kernel, act=act, has_bias=bias is not None, w_transposed=w_transposed),
        grid=(n // tn, m // tm),
        in_specs=in_specs,
        out_specs=pl.BlockSpec((tm, tn), lambda j, i: (i, j)),
        out_shape=jax.ShapeDtypeStruct((m, n), out_dtype),
        scratch_shapes=[pltpu.VMEM((k, tn), BF16)],
        compiler_params=_params(("parallel", "arbitrary")),
        name=name,
    )(*args)


def _merge_kernel(oa_ref, ob_ref, wa_ref, wb_ref, ga_ref, gb_ref, o_ref, wab_ref, wbb_ref):
    @pl.when(pl.program_id(1) == 0)
    def _():
        wab_ref[...] = wa_ref[...].astype(BF16)
        wbb_ref[...] = wb_ref[...].astype(BF16)

    ya = _dot(oa_ref[...], wab_ref[...])
    yb = _dot(ob_ref[...], wbb_ref[...])
    o_ref[...] = (ga_ref[...].astype(F32) * ya + gb_ref[...].astype(F32) * yb).astype(o_ref.dtype)


def _merge(o_a, o_b, wa, wb, gate, tm=512, tn=512):
    m, ka = o_a.shape
    kb = o_b.shape[1]
    n = wa.shape[1]
    off = n // tn
    return pl.pallas_call(
        _merge_kernel,
        grid=(n // tn, m // tm),
        in_specs=[pl.BlockSpec((tm, ka), lambda j, i: (i, 0)),
                  pl.BlockSpec((tm, kb), lambda j, i: (i, 0)),
                  pl.BlockSpec((ka, tn), lambda j, i: (0, j)),
                  pl.BlockSpec((kb, tn), lambda j, i: (0, j)),
                  pl.BlockSpec((tm, tn), lambda j, i: (i, j)),
                  pl.BlockSpec((tm, tn), lambda j, i: (i, j + off))],
        out_specs=pl.BlockSpec((tm, tn), lambda j, i: (i, j)),
        out_shape=jax.ShapeDtypeStruct((m, n), BF16),
        scratch_shapes=[pltpu.VMEM((ka, tn), BF16), pltpu.VMEM((kb, tn), BF16)],
        compiler_params=_params(("parallel", "arbitrary")),
        name="branch_merge",
    )(o_a, o_b, wa, wb, gate, gate)


def _resid_norm_kernel(a_ref, b_ref, x_ref, gt_ref, g_ref, *rest, affine, emit_x):
    rest = list(rest)
    if affine:
        sc_ref, sh_ref = rest[0], rest[1]
        rest = rest[2:]
    if emit_x:
        xo_ref, no_ref, acc_ref = rest
    else:
        no_ref, acc_ref = rest
    kk = pl.program_id(1)

    @pl.when(kk == 0)
    def _():
        acc_ref[...] = jnp.zeros_like(acc_ref)

    acc_ref[...] += _dot(a_ref[...], b_ref[...])

    @pl.when(kk == pl.num_programs(1) - 1)
    def _():
        xn = x_ref[...] + gt_ref[...] * acc_ref[...]
        if emit_x:
            xo_ref[...] = xn
        ms = jnp.mean(xn * xn, axis=-1, keepdims=True)
        y = xn * lax.rsqrt(ms + NORM_EPS) * g_ref[...]
        if affine:
            y = y * (1.0 + sc_ref[...]) + sh_ref[...]
        no_ref[...] = y.astype(no_ref.dtype)


def _resid_norm(a, b, x, gt, g, sc=None, sh=None, *, rows_per_batch, norm_dtype, emit_x, tm=512, tk=2048,
                name="resid_norm"):
    m, k = a.shape
    n = b.shape[1]
    bsz = gt.shape[0]
    tpb = rows_per_batch // tm
    affine = sc is not None
    vec = lambda v: v.reshape(bsz, 1, n)
    in_specs = [pl.BlockSpec((tm, tk), lambda i, kk: (i, kk)),
                pl.BlockSpec((tk, n), lambda i, kk: (kk, 0)),
                pl.BlockSpec((tm, n), lambda i, kk: (i, 0)),
                pl.BlockSpec((None, 1, n), lambda i, kk: (i // tpb, 0, 0)),
                pl.BlockSpec((1, n), lambda i, kk: (0, 0))]
    args = [a, b, x, vec(gt), g.reshape(1, n)]
    if affine:
        in_specs += [pl.BlockSpec((None, 1, n), lambda i, kk: (i // tpb, 0, 0))] * 2
        args += [vec(sc), vec(sh)]
    out_specs = [pl.BlockSpec((tm, n), lambda i, kk: (i, 0))]
    out_shape = [jax.ShapeDtypeStruct((m, n), norm_dtype)]
    if emit_x:
        out_specs = [pl.BlockSpec((tm, n), lambda i, kk: (i, 0))] + out_specs
        out_shape = [jax.ShapeDtypeStruct((m, n), F32)] + out_shape
    return pl.pallas_call(
        functools.partial(_resid_norm_kernel, affine=affine, emit_x=emit_x),
        grid=(m // tm, k // tk),
        in_specs=in_specs,
        out_specs=out_specs,
        out_shape=out_shape,
        scratch_shapes=[pltpu.VMEM((tm, n), F32)],
        compiler_params=_params(("parallel", "arbitrary")),
        name=name,
    )(*args)


def _seg_sum(x, ones_bd, terms):
    cols = x.shape[1]
    wid = ones_bd.shape[0]
    outs = [_dot_split(_dot, x[:, c:c + wid], ones_bd, True, terms=terms) for c in range(0, cols, wid)]
    return outs[0] if len(outs) == 1 else jnp.concatenate(outs, axis=1)


def _inv_unit_lower_all(a_list, ri, ci):
    n = a_list[0].shape[0]
    eye = (ri == ci).astype(F32)
    same16 = (ri // 16) == (ci // 16)
    same32 = (ri // 32) == (ci // 32)
    t = [eye + jnp.where(same16, a, 0.0) for a in a_list]
    p = [jnp.where(same16, a, 0.0).astype(BF16) for a in a_list]
    p = [_dot(x, x).astype(BF16) for x in p]
    for _ in range(2):
        r = [_dot(x, jnp.concatenate([tt.astype(BF16), x], axis=1)) for x, tt in zip(p, t)]
        t = [tt + rr[:, :n] for tt, rr in zip(t, r)]
        p = [rr[:, n:].astype(BF16) for rr in r]
    t = [tt + _dot(x, tt.astype(BF16)) for x, tt in zip(p, t)]
    for half, sel in ((16, same32 & (~same16)), (32, ~same32)):
        lo = [s for s in range(0, n, half) if (s // half) % 2 == 1]
        tb = [tt.astype(BF16) for tt in t]
        t_lo = [jnp.concatenate([tt[s:s + half] for s in lo], axis=0).astype(BF16) for tt in t]
        x = [_dot(tl, jnp.where(sel, a, 0.0).astype(BF16)).astype(BF16) for tl, a in zip(t_lo, a_list)]
        upd = [_dot(xx, b) for xx, b in zip(x, tb)]

        def add_rows(tt, u):
            parts = [tt[s:s + half] for s in range(0, n, half)]
            for j, s in enumerate(lo):
                parts[s // half] = parts[s // half] + u[j * half:(j + 1) * half]
            return jnp.concatenate(parts, axis=0)

        t = [add_rows(tt, u) for tt, u in zip(t, upd)]
    return t


def _rwkv_kernel(pr_ref, pk_ref, pv_ref, sm_ref, mur_ref, muk_ref, muv_ref, w0_ref, w2_ref, a0_ref, a2_ref,
                 g2_ref, kk_ref, ka_ref, rk_ref, lnw_ref, lnb_ref, o_ref, s_ref, prev_ref, y_ref):
    tb, hw = pr_ref.shape
    n_chunk = tb // RW_CHUNK
    n_pair = hw // LANES
    c_len = RW_CHUNK

    @pl.when(pl.program_id(2) == 0)
    def _():
        s_ref[...] = jnp.zeros_like(s_ref)
        prev_ref[...] = jnp.zeros_like(prev_ref)

    row = lax.broadcasted_iota(jnp.int32, (8, hw), 0)

    def shifted_mix(p_ref, mu_ref, slot):
        p = p_ref[...]
        ps = pltpu.roll(p, 1, 0)
        ps = jnp.concatenate([jnp.where(row == 0, prev_ref[slot:slot + 1, :], ps[0:8]), ps[8:]], axis=0)
        prev_ref[slot:slot + 1, :] = p[tb - 1:tb, :]
        return p + (ps - p) * mu_ref[...]

    r = shifted_mix(pr_ref, mur_ref, 0)
    k = shifted_mix(pk_ref, muk_ref, 1)
    v = shifted_mix(pv_ref, muv_ref, 2)

    lw = jnp.tanh(sm_ref[:, SM_W:SM_A]).astype(BF16)
    logw = (-math.exp(-0.5) * LOG2E) * _sigmoid(w0_ref[...] + _dot(lw, w2_ref[...]))
    a = _sigmoid(a0_ref[...] + _dot(sm_ref[:, SM_A:SM_G].astype(BF16), a2_ref[...]))
    g = _dot(_sigmoid(sm_ref[:, SM_G:SM_DT]).astype(BF16), g2_ref[...])

    li = lax.broadcasted_iota(jnp.int32, (LANES, LANES), 0)
    lj = lax.broadcasted_iota(jnp.int32, (LANES, LANES), 1)
    same_head = (li // HEAD) == (lj // HEAD)
    seg_w = min(2 * LANES, hw)
    si = lax.broadcasted_iota(jnp.int32, (seg_w, seg_w), 0)
    sj = lax.broadcasted_iota(jnp.int32, (seg_w, seg_w), 1)
    ones_bd = ((si // HEAD) == (sj // HEAD)).astype(BF16)

    kkv = k * kk_ref[...]
    kkn = kkv / jnp.maximum(jnp.sqrt(_seg_sum(kkv * kkv, ones_bd, 2)), 1e-12)
    k2 = k * (1.0 + (a - 1.0) * ka_ref[...])
    aa = -kkn
    bb = kkn * a
    bonus = _seg_sum(r * k2 * rk_ref[...], ones_bd, 1) * v

    ti = lax.broadcasted_iota(jnp.int32, (tb, tb), 0)
    tj = lax.broadcasted_iota(jnp.int32, (tb, tb), 1)
    cum_mat = (((ti // c_len) == (tj // c_len)) & (tj <= ti)).astype(BF16)
    lc = _dot_split(_dot, logw, cum_mat, False, terms=2)
    e_pos = jnp.exp2(lc)
    e_neg = jnp.exp2(-lc)
    rt = r * e_pos
    at = aa * jnp.exp2(lc - logw)
    bt = bb * e_neg
    kt = k2 * e_neg

    lane = lax.broadcasted_iota(jnp.int32, (c_len, LANES), 1)
    m0 = lane < HEAD

    def bd(x):
        return jnp.concatenate([jnp.where(m0, x, 0.0), jnp.where(m0, 0.0, x)], axis=0).astype(BF16)

    strict = same_head & ((li % HEAD) > (lj % HEAD))
    incl = same_head & ((li % HEAD) >= (lj % HEAD))

    inst = [(c, p) for c in range(n_chunk) for p in range(n_pair)]
    rows = lambda c: slice(c * c_len, (c + 1) * c_len)
    lanes = lambda p: slice(p * LANES, (p + 1) * LANES)
    tile = lambda x: [bd(x[rows(c), lanes(p)]) for c, p in inst]
    a_bd, r_bd, b_bd, k_bd, v_bd = tile(at), tile(rt), tile(bt), tile(kt), tile(v)

    bk_bd = [jnp.concatenate([z, w], axis=0) for z, w in zip(b_bd, k_bd)]
    gm = [_dot_nt(jnp.concatenate([x, y], axis=0), zw) for x, y, zw in zip(a_bd, r_bd, bk_bd)]
    a_ab = [jnp.where(strict, x[0:LANES, 0:LANES], 0.0) for x in gm]
    a_ak = [jnp.where(strict, x[0:LANES, LANES:], 0.0).astype(BF16) for x in gm]
    incl2 = jnp.concatenate([incl, incl], axis=1)
    a_rbk = [jnp.where(incl2, x[LANES:, :], 0.0).astype(BF16) for x in gm]
    t_inv = [x.astype(BF16) for x in _inv_unit_lower_all(a_ab, li, lj)]
    av = [_dot(x, y).astype(BF16) for x, y in zip(a_ak, v_bd)]
    tav = [_dot(t, jnp.concatenate([x, y], axis=1)) for t, x, y in zip(t_inv, a_bd, av)]
    ta = [x[:, :LANES].astype(BF16) for x in tav]
    uv = [x[:, LANES:] for x in tav]

    state = [s_ref[p] for p in range(n_pair)]
    for c in range(n_chunk):
        ids = [c * n_pair + p for p in range(n_pair)]
        s_b = [s.astype(BF16) for s in state]
        u_bd = [(_dot_nt(ta[i], s_b[p]) + uv[i]).astype(BF16) for p, i in enumerate(ids)]
        g_end = [e_pos[(c + 1) * c_len - 1:(c + 1) * c_len, lanes(p)] for p in range(n_pair)]
        uv_bd = [jnp.concatenate([u_bd[p], v_bd[i]], axis=0) for p, i in enumerate(ids)]
        state = [g_end[p] * (state[p] + _dot_tn(uv_bd[p], bk_bd[i])) for p, i in enumerate(ids)]
        for p, i in enumerate(ids):
            y_bd = _dot_nt(r_bd[i], s_b[p]) + _dot(a_rbk[i], uv_bd[p])
            y_ref[rows(c), lanes(p)] = y_bd[0:c_len] + y_bd[c_len:]
    for p in range(n_pair):
        s_ref[p] = state[p]

    y = y_ref[...]
    mu = _seg_sum(y, ones_bd, 1) * (1.0 / HEAD)
    d = y - mu
    var = _seg_sum(d * d, ones_bd, 1) * (1.0 / HEAD)
    yn = d * lax.rsqrt(var + RW_LNX_EPS) * lnw_ref[...] + lnb_ref[...]
    o_ref[...] = ((yn + bonus) * g).astype(o_ref.dtype)


def _rwkv_branch(p_rkv, small, mu_rkv, w0, w2, a0, a2, g2, k_k, k_a, r_k, lnx_w, lnx_b, tb=256, hw=1024):
    bsz, t, w3 = p_rkv.shape
    w = w3 // 3
    nb = w // hw
    row = lambda v: v.reshape(1, -1).astype(F32)
    vec_spec = lambda off: pl.BlockSpec((1, hw), lambda b, gi, ti: (0, gi + off))
    rkv_spec = lambda off: pl.BlockSpec((None, tb, hw), lambda b, gi, ti: (b, ti, gi + off))
    mat_spec = lambda rows: pl.BlockSpec((rows, hw), lambda b, gi, ti: (0, gi))
    mu = row(mu_rkv)
    return pl.pallas_call(
        _rwkv_kernel,
        grid=(bsz, nb, t // tb),
        in_specs=[rkv_spec(0), rkv_spec(nb), rkv_spec(2 * nb),
                  pl.BlockSpec((None, tb, SM_WIDTH), lambda b, gi, ti: (b, ti, 0)),
                  vec_spec(0), vec_spec(nb), vec_spec(2 * nb),
                  vec_spec(0), mat_spec(LANES), vec_spec(0), mat_spec(LANES), mat_spec(RW_GATE_LORA),
                  vec_spec(0), vec_spec(0), vec_spec(0), vec_spec(0), vec_spec(0)],
        out_specs=pl.BlockSpec((None, tb, hw), lambda b, gi, ti: (b, ti, gi)),
        out_shape=jax.ShapeDtypeStruct((bsz, t, w), BF16),
        scratch_shapes=[pltpu.VMEM((hw // LANES, LANES, LANES), F32),
                        pltpu.VMEM((8, hw), F32),
                        pltpu.VMEM((tb, hw), F32)],
        compiler_params=_params(("parallel", "parallel", "arbitrary")),
        name="rwkv7_branch",
    )(p_rkv, p_rkv, p_rkv, small, mu, mu, mu, row(w0), w2, row(a0), a2, g2,
      row(k_k), row(k_a), row(r_k), row(lnx_w), row(lnx_b))


def _ssd_kernel(xbc_ref, z_ref, dt_ref, cw_ref, cb_ref, dtb_ref, alog_ref, dsk_ref, nw_ref, exp_ref,
                o_ref, ext_ref, act_ref, st_ref):
    ch, cw = xbc_ref.shape
    width = z_ref.shape[1]
    gw = width // MB_GROUPS
    tail = 8

    @pl.when(pl.program_id(1) == 0)
    def _():
        st_ref[...] = jnp.zeros_like(st_ref)
        ext_ref[0:tail, :] = jnp.zeros((tail, cw), F32)

    @pl.when(pl.program_id(1) != 0)
    def _():
        ext_ref[0:tail, :] = ext_ref[ch:ch + tail, :]

    ext_ref[tail:tail + ch, :] = xbc_ref[...]

    cblk = 512
    for c0 in range(0, cw, cblk):
        e = ext_ref[:, c0:c0 + cblk]
        acc = cb_ref[:, c0:c0 + cblk] + cw_ref[MB_CONV - 1:MB_CONV, c0:c0 + cblk] * e[tail:]
        for s in range(1, MB_CONV):
            acc = acc + cw_ref[MB_CONV - 1 - s:MB_CONV - s, c0:c0 + cblk] * pltpu.roll(e, s, 0)[tail:]
        act_ref[:, c0:c0 + cblk] = acc * _sigmoid(acc)

    dt = _softplus(dt_ref[...] + dtb_ref[...])
    d_a = dt * (-jnp.exp(alog_ref[...])) * LOG2E
    ti = lax.broadcasted_iota(jnp.int32, (ch, ch), 0)
    tj = lax.broadcasted_iota(jnp.int32, (ch, ch), 1)
    causal = tj <= ti
    acs = _dot_split(_dot, d_a, causal.astype(BF16), False)
    acs_t = _dot_split(_dot_tn, d_a, (ti <= tj).astype(BF16), True)
    expand = exp_ref[...]
    dt_x = _dot_split(_dot, dt, expand, True)
    acs_x = _dot_split(_dot, acs, expand, True)
    acs_last_x = acs_x[ch - 1:ch, :]
    lane = lax.broadcasted_iota(jnp.int32, (ch, LANES), 1)
    m0 = lane < HEAD

    for gi in range(MB_GROUPS):
        gs = slice(gi * gw, (gi + 1) * gw)
        c_g = act_ref[:, width + MB_GROUPS * MB_STATE + gi * MB_STATE:width + MB_GROUPS * MB_STATE + (gi + 1) * MB_STATE].astype(BF16)
        b_g = act_ref[:, width + gi * MB_STATE:width + (gi + 1) * MB_STATE].astype(BF16)
        xs_g = act_ref[:, gs]
        acs_g = acs_x[:, gs]
        x_g = xs_g * dt_x[:, gs]
        cb = _dot_nt(c_g, b_g)
        s_old = st_ref[gi]
        y_off = _dot(c_g, s_old.astype(BF16)) * jnp.exp2(acs_g)
        y_diag = []
        for q in range(gw // LANES):
            x_pair = x_g[:, q * LANES:(q + 1) * LANES]
            y_pair = None
            for hh in range(2):
                head = (gi * gw) // HEAD + 2 * q + hh
                col = jnp.broadcast_to(acs[:, head:head + 1], (ch, ch))
                rowv = acs_t[head:head + 1, :]
                l_dec = jnp.exp2(jnp.where(causal, col - rowv, -jnp.inf))
                scores = (cb * l_dec).astype(BF16)
                x_m = (jnp.where(m0, x_pair, 0.0) if hh == 0 else jnp.where(m0, 0.0, x_pair)).astype(BF16)
                part = _dot(scores, x_m)
                y_pair = part if y_pair is None else y_pair + part
            y_diag.append(y_pair)
        y = jnp.concatenate(y_diag, axis=1) + y_off + xs_g * dsk_ref[:, gs]
        last_g = acs_last_x[:, gs]
        x_dec = (x_g * jnp.exp2(last_g - acs_g)).astype(BF16)
        st_ref[gi] = s_old * jnp.exp2(last_g) + _dot_tn(b_g, x_dec)
        z = z_ref[:, gs]
        yg = y * (z * _sigmoid(z))
        ms = jnp.mean(yg * yg, axis=-1, keepdims=True)
        o_ref[:, gs] = (yg * lax.rsqrt(ms + NORM_EPS) * nw_ref[:, gs]).astype(o_ref.dtype)


def _ssd_branch(p_z, p_xbc, small, conv_w, conv_b, dt_bias, a_log, d_skip, norm_w):
    bsz, t, width = p_z.shape
    cw = p_xbc.shape[2]
    heads = width // HEAD
    ch = MB_CHUNK
    pad = lambda v: jnp.zeros((1, LANES), F32).at[0, :heads].set(v.astype(F32))
    hid = jnp.arange(LANES)[:, None]
    col = jnp.arange(width)[None, :]
    expand = (hid == col // HEAD).astype(BF16)
    d_x = jnp.repeat(d_skip.astype(F32), HEAD).reshape(1, width)
    full = lambda shape: pl.BlockSpec(shape, lambda b, c: (0, 0))
    return pl.pallas_call(
        _ssd_kernel,
        grid=(bsz, t // ch),
        in_specs=[pl.BlockSpec((None, ch, cw), lambda b, c: (b, c, 0)),
                  pl.BlockSpec((None, ch, width), lambda b, c: (b, c, 0)),
                  pl.BlockSpec((None, ch, LANES), lambda b, c: (b, c, SM_DT // LANES)),
                  full((MB_CONV, cw)), full((1, cw)), full((1, LANES)), full((1, LANES)),
                  full((1, width)), full((1, width)), full((LANES, width))],
        out_specs=pl.BlockSpec((None, ch, width), lambda b, c: (b, c, 0)),
        out_shape=jax.ShapeDtypeStruct((bsz, t, width), BF16),
        scratch_shapes=[pltpu.VMEM((ch + 8, cw), F32),
                        pltpu.VMEM((ch, cw), F32),
                        pltpu.VMEM((MB_GROUPS, MB_STATE, width // MB_GROUPS), F32)],
        compiler_params=_params(("parallel", "arbitrary")),
        name="mamba2_branch",
    )(p_xbc, p_z, small, conv_w.astype(F32), conv_b.reshape(1, cw).astype(F32), pad(dt_bias), pad(a_log),
      d_x, norm_w.reshape(1, width).astype(F32), expand)


def _pad_cols(w, n):
    return jnp.zeros((w.shape[0], n), w.dtype).at[:, :w.shape[1]].set(w)


def _pad_rows(w, n):
    return jnp.zeros((n, w.shape[1]), w.dtype).at[:w.shape[0]].set(w)


def _layer(x, c_act_mod, l, norm1_g, w_in, rw_mu_rkv, rw_mu_wag, rw_w0, rw_w1, rw_w2, rw_a0, rw_a1, rw_a2, rw_g1,
           rw_g2, rw_k_k, rw_k_a, rw_r_k, rw_lnx_w, rw_lnx_b, mb_conv_w, mb_conv_b, mb_dt_bias, mb_a_log, mb_d,
           mb_norm_w, w_branch_a, w_branch_b, w_gate, b_gate, w_out, norm2_g, mlp_up, mlp_down, out_g, out_affine):
    bsz, t, d = x.shape
    m = bsz * t
    w = rw_k_k.shape[1]
    mbw = mb_norm_w.shape[1]
    cw = mb_conv_b.shape[1]
    off_rkv, off_z, off_xbc = 3 * w, 3 * w + mbw, 3 * w + mbw + cw
    sh1, sc1, gt1, sh2, sc2, gt2 = jnp.split(c_act_mod, 6, axis=-1)

    w_small = jnp.concatenate([_pad_cols(rw_w1[l], LANES), _pad_cols(rw_a1[l], LANES), rw_g1[l],
                               _pad_cols(w_in[l][:, off_xbc:], LANES)], axis=1).astype(BF16)
    h, small = _norm1(x, norm1_g[l], sc1, sh1, rw_mu_wag[l], w_small)
    h2d = h.reshape(m, d)
    w_in_t = jnp.swapaxes(w_in[l], 0, 1)
    p_rkv = _matmul(h2d, w_in_t, F32, 0, off_rkv, w_transposed=True, name="proj_rkv").reshape(bsz, t, off_rkv)
    p_z = _matmul(h2d, w_in_t, F32, off_rkv, mbw, w_transposed=True, name="proj_z").reshape(bsz, t, mbw)
    p_xbc = _matmul(h2d, w_in_t, F32, off_z, cw, w_transposed=True, name="proj_xbc").reshape(bsz, t, cw)
    gate = _matmul(h2d, w_gate[l], BF16, bias=b_gate[l], act="sigmoid", name="proj_gate")

    o_a = _rwkv_branch(p_rkv, small, rw_mu_rkv[l], rw_w0[l], _pad_rows(rw_w2[l], LANES).astype(BF16), rw_a0[l],
                       _pad_rows(rw_a2[l], LANES).astype(BF16), rw_g2[l].astype(BF16), rw_k_k[l], rw_k_a[l],
                       rw_r_k[l], rw_lnx_w[l], rw_lnx_b[l])
    o_b = _ssd_branch(p_z, p_xbc, small, mb_conv_w[l], mb_conv_b[l], mb_dt_bias[l], mb_a_log[l], mb_d[l],
                      mb_norm_w[l])
    merged = _merge(o_a.reshape(m, w), o_b.reshape(m, mbw), w_branch_a[l], w_branch_b[l], gate)
    x1, h2 = _resid_norm(merged, w_out[l].astype(BF16), x.reshape(m, d), gt1, norm2_g[l], sc2, sh2,
                         rows_per_batch=t, norm_dtype=BF16, emit_x=True, name="out_proj_norm2")
    u = _matmul(h2, mlp_up[l], BF16, act="relu2", name="mlp_up")
    if out_affine is None:
        (y,) = _resid_norm(u, mlp_down[l].astype(BF16), x1, gt2, out_g, rows_per_batch=t, norm_dtype=F32,
                           emit_x=False, name="mlp_down_final_norm")
        return y.reshape(bsz, t, d)
    raise NotImplementedError("only a single layer followed by the final norm is implemented")


def kernel(x, c, ada_w, ada_b, norm1_g, w_in, rw_mu_rkv, rw_mu_wag, rw_w0, rw_w1, rw_w2, rw_a0, rw_a1, rw_a2, rw_g1, rw_g2, rw_k_k, rw_k_a, rw_r_k, rw_lnx_w, rw_lnx_b, mb_conv_w, mb_conv_b, mb_dt_bias, mb_a_log, mb_d, mb_norm_w, w_branch_a, w_branch_b, w_gate, b_gate, w_out, norm2_g, mlp_up, mlp_down, final_norm_g):
    depth = ada_w.shape[0]
    assert depth == 1, "the fused final-norm epilogue assumes a single layer"
    mod = _ada_mod(c, ada_w[0], ada_b[0])
    return _layer(x, mod, 0, norm1_g, w_in, rw_mu_rkv, rw_mu_wag, rw_w0, rw_w1, rw_w2, rw_a0, rw_a1, rw_a2, rw_g1,
                  rw_g2, rw_k_k, rw_k_a, rw_r_k, rw_lnx_w, rw_lnx_b, mb_conv_w, mb_conv_b, mb_dt_bias, mb_a_log,
                  mb_d, mb_norm_w, w_branch_a, w_branch_b, w_gate, b_gate, w_out, norm2_g, mlp_up, mlp_down,
                  final_norm_g, None)
```

```python
import functools
import math

import jax
import jax.numpy as jnp
from jax import lax
from jax.experimental import pallas as pl
from jax.experimental.pallas import tpu as pltpu

F32 = jnp.float32
BF16 = jnp.bfloat16

LANES = 128
HEAD = 64
RW_CHUNK = 64
RW_DECAY_LORA = 96
RW_AAA_LORA = 96
RW_GATE_LORA = 256
RW_LNX_EPS = 64e-5
MB_GROUPS = 8
MB_STATE = 128
MB_CONV = 4
MB_CHUNK = 128
NORM_EPS = 1e-5
LOG2E = 1.4426950408889634
VMEM_LIMIT = 56 * 1024 * 1024

SM_W, SM_A, SM_G, SM_DT, SM_WIDTH = 0, 128, 256, 512, 640


def _params(sem):
    return pltpu.CompilerParams(dimension_semantics=sem, vmem_limit_bytes=VMEM_LIMIT)


def _sigmoid(x):
    return 1.0 / (1.0 + jnp.exp(-x))


def _softplus(x):
    return jnp.maximum(x, 0.0) + jnp.log1p(jnp.exp(-jnp.abs(x)))


def _dot(a, b):
    return jnp.dot(a, b, preferred_element_type=F32)


def _dot_nt(a, b):
    return lax.dot_general(a, b, (((1,), (1,)), ((), ())), preferred_element_type=F32)


def _dot_tn(a, b):
    return lax.dot_general(a, b, (((0,), (0,)), ((), ())), preferred_element_type=F32)


def _split_bf16(x, terms):
    parts = []
    for _ in range(terms - 1):
        hi = x.astype(BF16)
        parts.append(hi)
        x = x - hi.astype(F32)
    parts.append(x.astype(BF16))
    return parts


def _dot_split(dot_fn, x, m, x_is_lhs, terms=3):
    parts = _split_bf16(x, terms)
    prods = [dot_fn(p, m) if x_is_lhs else dot_fn(m, p) for p in parts]
    out = prods[0]
    for p in prods[1:]:
        out = out + p
    return out


def _ada_kernel(c_ref, w_ref, b_ref, o_ref):
    c = c_ref[...]
    ca = (c * _sigmoid(c)).astype(BF16)
    o_ref[...] = _dot(ca, w_ref[...].astype(BF16)) + b_ref[...]


def _ada_mod(c, ada_w, ada_b):
    bsz, d = c.shape
    n = ada_w.shape[1]
    rows = 8
    c_pad = jnp.zeros((rows, d), F32).at[:bsz].set(c)
    tn = 1024
    out = pl.pallas_call(
        _ada_kernel,
        grid=(n // tn,),
        in_specs=[pl.BlockSpec((rows, d), lambda j: (0, 0)),
                  pl.BlockSpec((d, tn), lambda j: (0, j)),
                  pl.BlockSpec((1, tn), lambda j: (0, j))],
        out_specs=pl.BlockSpec((rows, tn), lambda j: (0, j)),
        out_shape=jax.ShapeDtypeStruct((rows, n), F32),
        compiler_params=_params(("parallel",)),
        name="ada_mod",
    )(c_pad, ada_w, ada_b.reshape(1, n))
    return out[:bsz]


def _norm1_kernel(x_ref, g_ref, sc_ref, sh_ref, mu_ref, ws_ref, h_ref, sm_ref, prev_ref):
    i = pl.program_id(1)

    @pl.when(i == 0)
    def _():
        prev_ref[...] = jnp.zeros_like(prev_ref)

    x = x_ref[...]
    tm = x.shape[0]
    ms = jnp.mean(x * x, axis=-1, keepdims=True)
    h = x * lax.rsqrt(ms + NORM_EPS) * g_ref[...] * (1.0 + sc_ref[...]) + sh_ref[...]
    hs = pltpu.roll(h, 1, 0)
    row = lax.broadcasted_iota(jnp.int32, (8, h.shape[1]), 0)
    hs = jnp.concatenate([jnp.where(row == 0, prev_ref[0:1, :], hs[0:8]), hs[8:]], axis=0)
    prev_ref[0:1, :] = h[tm - 1:tm, :]
    dh = hs - h
    hb = h.astype(BF16)
    h_ref[...] = hb
    xw = (h + dh * mu_ref[0:1, :]).astype(BF16)
    sm_ref[:, SM_W:SM_A] = _dot(xw, ws_ref[:, SM_W:SM_A])
    xa = (h + dh * mu_ref[1:2, :]).astype(BF16)
    sm_ref[:, SM_A:SM_G] = _dot(xa, ws_ref[:, SM_A:SM_G])
    xg = (h + dh * mu_ref[2:3, :]).astype(BF16)
    sm_ref[:, SM_G:SM_DT] = _dot(xg, ws_ref[:, SM_G:SM_DT])
    sm_ref[:, SM_DT:SM_WIDTH] = _dot(hb, ws_ref[:, SM_DT:SM_WIDTH])


def _norm1(x, g, sc, sh, mu_wag, w_small, tm=256):
    bsz, t, d = x.shape
    return pl.pallas_call(
        _norm1_kernel,
        grid=(bsz, t // tm),
        in_specs=[pl.BlockSpec((None, tm, d), lambda b, i: (b, i, 0)),
                  pl.BlockSpec((1, d), lambda b, i: (0, 0)),
                  pl.BlockSpec((None, 1, d), lambda b, i: (b, 0, 0)),
                  pl.BlockSpec((None, 1, d), lambda b, i: (b, 0, 0)),
                  pl.BlockSpec((3, d), lambda b, i: (0, 0)),
                  pl.BlockSpec((d, SM_WIDTH), lambda b, i: (0, 0))],
        out_specs=[pl.BlockSpec((None, tm, d), lambda b, i: (b, i, 0)),
                   pl.BlockSpec((None, tm, SM_WIDTH), lambda b, i: (b, i, 0))],
        out_shape=[jax.ShapeDtypeStruct((bsz, t, d), BF16),
                   jax.ShapeDtypeStruct((bsz, t, SM_WIDTH), F32)],
        scratch_shapes=[pltpu.VMEM((8, d), F32)],
        compiler_params=_params(("parallel", "arbitrary")),
        name="norm1_small_proj",
    )(x, g.reshape(1, d), sc.reshape(bsz, 1, d), sh.reshape(bsz, 1, d), mu_wag, w_small)


def _mm_kernel(a_ref, w_ref, *rest, act, has_bias, has_side, w_transposed):
    rest = list(rest)
    bias_ref = rest.pop(0) if has_bias else None
    side_ref = rest.pop(0) if has_side else None
    o_ref = rest.pop(0)
    side_o_ref = rest.pop(0) if has_side else None
    (wb_ref,) = rest

    @pl.when(pl.program_id(1) == 0)
    def _():
        w = w_ref[...]
        wb_ref[...] = (w.T if w_transposed else w).astype(BF16)

    if has_side:
        side_o_ref[...] = side_ref[...].astype(BF16)

    y = _dot(a_ref[...], wb_ref[...])
    if has_bias:
        y = y + bias_ref[...]
    if act == "sigmoid":
        y = _sigmoid(y)
    elif act == "relu2":
        y = jnp.maximum(y, 0.0)
        y = y * y
    o_ref[...] = y.astype(o_ref.dtype)


def _matmul(a, w, out_dtype, col0=0, n=None, bias=None, act=None, w_transposed=False, side_cast=None, tm=1024,
            tn=1024, name="matmul"):
    m, k = a.shape
    n = w.shape[0 if w_transposed else 1] if n is None else n
    tm, tn = min(tm, m), min(tn, n)
    assert col0 % tn == 0 and n % tn == 0 and m % tm == 0
    jb = col0 // tn
    w_spec = (pl.BlockSpec((tn, k), lambda j, i: (j + jb, 0)) if w_transposed
              else pl.BlockSpec((k, tn), lambda j, i: (0, j + jb)))
    in_specs = [pl.BlockSpec((tm, k), lambda j, i: (i, 0)), w_spec]
    args = [a, w]
    if bias is not None:
        in_specs.append(pl.BlockSpec((1, tn), lambda j, i: (0, j)))
        args.append(bias.reshape(1, n).astype(F32))
    out_specs = [pl.BlockSpec((tm, tn), lambda j, i: (i, j))]
    out_shape = [jax.ShapeDtypeStruct((m, n), out_dtype)]
    if side_cast is not None:
        ni = m // tm
        rows = side_cast.shape[0] // ((n // tn) * ni)
        assert rows * (n // tn) * ni == side_cast.shape[0] and rows % 16 == 0
        side_spec = pl.BlockSpec((rows, side_cast.shape[1]), lambda j, i: (j * ni + i, 0))
        in_specs.append(side_spec)
        args.append(side_cast)
        out_specs.append(side_spec)
        out_shape.append(jax.ShapeDtypeStruct(side_cast.shape, BF16))
    outs = pl.pallas_call(
        functools.partial(_mm_kernel, act=act, has_bias=bias is not None, has_side=side_cast is not None,
                          w_transposed=w_transposed),
        grid=(n // tn, m // tm),
        in_specs=in_specs,
        out_specs=out_specs,
        out_shape=out_shape,
        scratch_shapes=[pltpu.VMEM((k, tn), BF16)],
        compiler_params=_params(("parallel", "arbitrary")),
        name=name,
    )(*args)
    return outs if side_cast is not None else outs[0]


def _merge_kernel(oa_ref, ob_ref, wa_ref, wb_ref, ga_ref, gb_ref, o_ref, wab_ref, wbb_ref):
    @pl.when(pl.program_id(1) == 0)
    def _():
        wab_ref[...] = wa_ref[...].astype(BF16)
        wbb_ref[...] = wb_ref[...].astype(BF16)

    ya = _dot(oa_ref[...], wab_ref[...])
    yb = _dot(ob_ref[...], wbb_ref[...])
    o_ref[...] = (ga_ref[...].astype(F32) * ya + gb_ref[...].astype(F32) * yb).astype(o_ref.dtype)


def _merge(o_a, o_b, wa, wb, gate, tm=512, tn=512):
    m, ka = o_a.shape
    kb = o_b.shape[1]
    n = wa.shape[1]
    off = n // tn
    return pl.pallas_call(
        _merge_kernel,
        grid=(n // tn, m // tm),
        in_specs=[pl.BlockSpec((tm, ka), lambda j, i: (i, 0)),
                  pl.BlockSpec((tm, kb), lambda j, i: (i, 0)),
                  pl.BlockSpec((ka, tn), lambda j, i: (0, j)),
                  pl.BlockSpec((kb, tn), lambda j, i: (0, j)),
                  pl.BlockSpec((tm, tn), lambda j, i: (i, j)),
                  pl.BlockSpec((tm, tn), lambda j, i: (i, j + off))],
        out_specs=pl.BlockSpec((tm, tn), lambda j, i: (i, j)),
        out_shape=jax.ShapeDtypeStruct((m, n), BF16),
        scratch_shapes=[pltpu.VMEM((ka, tn), BF16), pltpu.VMEM((kb, tn), BF16)],
        compiler_params=_params(("parallel", "arbitrary")),
        name="branch_merge",
    )(o_a, o_b, wa, wb, gate, gate)


def _resid_norm_kernel(a_ref, b_ref, x_ref, gt_ref, g_ref, *rest, affine, emit_x):
    rest = list(rest)
    if affine:
        sc_ref, sh_ref = rest[0], rest[1]
        rest = rest[2:]
    if emit_x:
        xo_ref, no_ref, acc_ref = rest
    else:
        no_ref, acc_ref = rest
    kk = pl.program_id(1)

    @pl.when(kk == 0)
    def _():
        acc_ref[...] = jnp.zeros_like(acc_ref)

    acc_ref[...] += _dot(a_ref[...], b_ref[...])

    @pl.when(kk == pl.num_programs(1) - 1)
    def _():
        xn = x_ref[...] + gt_ref[...] * acc_ref[...]
        if emit_x:
            xo_ref[...] = xn
        ms = jnp.mean(xn * xn, axis=-1, keepdims=True)
        y = xn * lax.rsqrt(ms + NORM_EPS) * g_ref[...]
        if affine:
            y = y * (1.0 + sc_ref[...]) + sh_ref[...]
        no_ref[...] = y.astype(no_ref.dtype)


def _resid_norm(a, b, x, gt, g, sc=None, sh=None, *, rows_per_batch, norm_dtype, emit_x, tm=512, tk=2048,
                name="resid_norm"):
    m, k = a.shape
    n = b.shape[1]
    bsz = gt.shape[0]
    tpb = rows_per_batch // tm
    affine = sc is not None
    vec = lambda v: v.reshape(bsz, 1, n)
    in_specs = [pl.BlockSpec((tm, tk), lambda i, kk: (i, kk)),
                pl.BlockSpec((tk, n), lambda i, kk: (kk, 0)),
                pl.BlockSpec((tm, n), lambda i, kk: (i, 0)),
                pl.BlockSpec((None, 1, n), lambda i, kk: (i // tpb, 0, 0)),
                pl.BlockSpec((1, n), lambda i, kk: (0, 0))]
    args = [a, b, x, vec(gt), g.reshape(1, n)]
    if affine:
        in_specs += [pl.BlockSpec((None, 1, n), lambda i, kk: (i // tpb, 0, 0))] * 2
        args += [vec(sc), vec(sh)]
    out_specs = [pl.BlockSpec((tm, n), lambda i, kk: (i, 0))]
    out_shape = [jax.ShapeDtypeStruct((m, n), norm_dtype)]
    if emit_x:
        out_specs = [pl.BlockSpec((tm, n), lambda i, kk: (i, 0))] + out_specs
        out_shape = [jax.ShapeDtypeStruct((m, n), F32)] + out_shape
    return pl.pallas_call(
        functools.partial(_resid_norm_kernel, affine=affine, emit_x=emit_x),
        grid=(m // tm, k // tk),
        in_specs=in_specs,
        out_specs=out_specs,
        out_shape=out_shape,
        scratch_shapes=[pltpu.VMEM((tm, n), F32)],
        compiler_params=_params(("parallel", "arbitrary")),
        name=name,
    )(*args)


def _seg_sum(x, ones_bd, terms):
    cols = x.shape[1]
    wid = ones_bd.shape[0]
    outs = [_dot_split(_dot, x[:, c:c + wid], ones_bd, True, terms=terms) for c in range(0, cols, wid)]
    return outs[0] if len(outs) == 1 else jnp.concatenate(outs, axis=1)


def _inv_unit_lower_all(a_list, ri, ci):
    n = a_list[0].shape[0]
    eye = (ri == ci).astype(F32)
    same16 = (ri // 16) == (ci // 16)
    same32 = (ri // 32) == (ci // 32)
    t = [eye + jnp.where(same16, a, 0.0) for a in a_list]
    p = [jnp.where(same16, a, 0.0).astype(BF16) for a in a_list]
    p = [_dot(x, x).astype(BF16) for x in p]
    for _ in range(2):
        r = [_dot(x, jnp.concatenate([tt.astype(BF16), x], axis=1)) for x, tt in zip(p, t)]
        t = [tt + rr[:, :n] for tt, rr in zip(t, r)]
        p = [rr[:, n:].astype(BF16) for rr in r]
    t = [tt + _dot(x, tt.astype(BF16)) for x, tt in zip(p, t)]
    for half, sel in ((16, same32 & (~same16)), (32, ~same32)):
        lo = [s for s in range(0, n, half) if (s // half) % 2 == 1]
        tb = [tt.astype(BF16) for tt in t]
        t_lo = [jnp.concatenate([tt[s:s + half] for s in lo], axis=0).astype(BF16) for tt in t]
        x = [_dot(tl, jnp.where(sel, a, 0.0).astype(BF16)).astype(BF16) for tl, a in zip(t_lo, a_list)]
        upd = [_dot(xx, b) for xx, b in zip(x, tb)]

        def add_rows(tt, u):
            parts = [tt[s:s + half] for s in range(0, n, half)]
            for j, s in enumerate(lo):
                parts[s // half] = parts[s // half] + u[j * half:(j + 1) * half]
            return jnp.concatenate(parts, axis=0)

        t = [add_rows(tt, u) for tt, u in zip(t, upd)]
    return t


def _rwkv_kernel(pr_ref, pk_ref, pv_ref, sm_ref, mur_ref, muk_ref, muv_ref, w0_ref, w2_ref, a0_ref, a2_ref,
                 g2_ref, kk_ref, ka_ref, rk_ref, lnw_ref, lnb_ref, o_ref, s_ref, prev_ref, y_ref):
    tb, hw = pr_ref.shape
    n_chunk = tb // RW_CHUNK
    n_pair = hw // LANES
    c_len = RW_CHUNK

    @pl.when(pl.program_id(2) == 0)
    def _():
        s_ref[...] = jnp.zeros_like(s_ref)
        prev_ref[...] = jnp.zeros_like(prev_ref)

    row = lax.broadcasted_iota(jnp.int32, (8, hw), 0)

    def shifted_mix(p_ref, mu_ref, slot):
        p = p_ref[...]
        ps = pltpu.roll(p, 1, 0)
        ps = jnp.concatenate([jnp.where(row == 0, prev_ref[slot:slot + 1, :], ps[0:8]), ps[8:]], axis=0)
        prev_ref[slot:slot + 1, :] = p[tb - 1:tb, :]
        return p + (ps - p) * mu_ref[...]

    r = shifted_mix(pr_ref, mur_ref, 0)
    k = shifted_mix(pk_ref, muk_ref, 1)
    v = shifted_mix(pv_ref, muv_ref, 2)

    lw = jnp.tanh(sm_ref[:, SM_W:SM_A]).astype(BF16)
    logw = (-math.exp(-0.5) * LOG2E) * _sigmoid(w0_ref[...] + _dot(lw, w2_ref[...]))
    a = _sigmoid(a0_ref[...] + _dot(sm_ref[:, SM_A:SM_G].astype(BF16), a2_ref[...]))
    g = _dot(_sigmoid(sm_ref[:, SM_G:SM_DT]).astype(BF16), g2_ref[...])

    li = lax.broadcasted_iota(jnp.int32, (LANES, LANES), 0)
    lj = lax.broadcasted_iota(jnp.int32, (LANES, LANES), 1)
    same_head = (li // HEAD) == (lj // HEAD)
    seg_w = min(2 * LANES, hw)
    si = lax.broadcasted_iota(jnp.int32, (seg_w, seg_w), 0)
    sj = lax.broadcasted_iota(jnp.int32, (seg_w, seg_w), 1)
    ones_bd = ((si // HEAD) == (sj // HEAD)).astype(BF16)

    kkv = k * kk_ref[...]
    kkn = kkv * lax.rsqrt(jnp.maximum(_seg_sum(kkv * kkv, ones_bd, 2), 1e-24))
    k2 = k * (1.0 + (a - 1.0) * ka_ref[...])
    aa = -kkn
    bb = kkn * a
    bonus = _seg_sum(r * k2 * rk_ref[...], ones_bd, 1) * v

    ti = lax.broadcasted_iota(jnp.int32, (tb, tb), 0)
    tj = lax.broadcasted_iota(jnp.int32, (tb, tb), 1)
    cum_mat = (((ti // c_len) == (tj // c_len)) & (tj <= ti)).astype(BF16)
    lc = _dot_split(_dot, logw, cum_mat, False, terms=2)
    e_pos = jnp.exp2(lc)
    e_neg = jnp.exp2(-lc)
    rt = r * e_pos
    at = aa * jnp.exp2(lc - logw)
    bt = bb * e_neg
    kt = k2 * e_neg

    lane = lax.broadcasted_iota(jnp.int32, (c_len, LANES), 1)
    m0 = lane < HEAD

    def bd(x):
        xb = x.astype(BF16)
        zero = jnp.zeros_like(xb)
        return jnp.concatenate([jnp.where(m0, xb, zero), jnp.where(m0, zero, xb)], axis=0)

    strict = same_head & ((li % HEAD) > (lj % HEAD))
    incl = same_head & ((li % HEAD) >= (lj % HEAD))

    inst = [(c, p) for c in range(n_chunk) for p in range(n_pair)]
    rows = lambda c: slice(c * c_len, (c + 1) * c_len)
    lanes = lambda p: slice(p * LANES, (p + 1) * LANES)
    tile = lambda x: [bd(x[rows(c), lanes(p)]) for c, p in inst]
    a_bd, r_bd, b_bd, k_bd, v_bd = tile(at), tile(rt), tile(bt), tile(kt), tile(v)

    bk_bd = [jnp.concatenate([z, w], axis=0) for z, w in zip(b_bd, k_bd)]
    gm = [_dot_nt(jnp.concatenate([x, y], axis=0), zw) for x, y, zw in zip(a_bd, r_bd, bk_bd)]
    a_ab = [jnp.where(strict, x[0:LANES, 0:LANES], 0.0) for x in gm]
    a_ak = [jnp.where(strict, x[0:LANES, LANES:], 0.0).astype(BF16) for x in gm]
    incl2 = jnp.concatenate([incl, incl], axis=1)
    a_rbk = [jnp.where(incl2, x[LANES:, :], 0.0).astype(BF16) for x in gm]
    t_inv = [x.astype(BF16) for x in _inv_unit_lower_all(a_ab, li, lj)]
    av = [_dot(x, y).astype(BF16) for x, y in zip(a_ak, v_bd)]
    tav = [_dot(t, jnp.concatenate([x, y], axis=1)) for t, x, y in zip(t_inv, a_bd, av)]
    ta = [x[:, :LANES].astype(BF16) for x in tav]
    uv = [x[:, LANES:] for x in tav]

    state = [s_ref[p] for p in range(n_pair)]
    for c in range(n_chunk):
        ids = [c * n_pair + p for p in range(n_pair)]
        s_b = [s.astype(BF16) for s in state]
        u_bd = [(_dot_nt(ta[i], s_b[p]) + uv[i]).astype(BF16) for p, i in enumerate(ids)]
        g_end = [e_pos[(c + 1) * c_len - 1:(c + 1) * c_len, lanes(p)] for p in range(n_pair)]
        uv_bd = [jnp.concatenate([u_bd[p], v_bd[i]], axis=0) for p, i in enumerate(ids)]
        state = [g_end[p] * (state[p] + _dot_tn(uv_bd[p], bk_bd[i])) for p, i in enumerate(ids)]
        for p, i in enumerate(ids):
            y_bd = _dot_nt(r_bd[i], s_b[p]) + _dot(a_rbk[i], uv_bd[p])
            y_ref[rows(c), lanes(p)] = y_bd[0:c_len] + y_bd[c_len:]
    for p in range(n_pair):
        s_ref[p] = state[p]

    y = y_ref[...]
    mu = _seg_sum(y, ones_bd, 1) * (1.0 / HEAD)
    d = y - mu
    var = _seg_sum(d * d, ones_bd, 1) * (1.0 / HEAD)
    yn = d * lax.rsqrt(var + RW_LNX_EPS) * lnw_ref[...] + lnb_ref[...]
    o_ref[...] = ((yn + bonus) * g).astype(o_ref.dtype)


def _rwkv_branch(p_rkv, small, mu_rkv, w0, w2, a0, a2, g2, k_k, k_a, r_k, lnx_w, lnx_b, tb=256, hw=1024):
    bsz, t, w3 = p_rkv.shape
    w = w3 // 3
    nb = w // hw
    row = lambda v: v.reshape(1, -1).astype(F32)
    vec_spec = lambda off: pl.BlockSpec((1, hw), lambda b, gi, ti: (0, gi + off))
    rkv_spec = lambda off: pl.BlockSpec((None, tb, hw), lambda b, gi, ti: (b, ti, gi + off))
    mat_spec = lambda rows: pl.BlockSpec((rows, hw), lambda b, gi, ti: (0, gi))
    mu = row(mu_rkv)
    return pl.pallas_call(
        _rwkv_kernel,
        grid=(bsz, nb, t // tb),
        in_specs=[rkv_spec(0), rkv_spec(nb), rkv_spec(2 * nb),
                  pl.BlockSpec((None, tb, SM_WIDTH), lambda b, gi, ti: (b, ti, 0)),
                  vec_spec(0), vec_spec(nb), vec_spec(2 * nb),
                  vec_spec(0), mat_spec(LANES), vec_spec(0), mat_spec(LANES), mat_spec(RW_GATE_LORA),
                  vec_spec(0), vec_spec(0), vec_spec(0), vec_spec(0), vec_spec(0)],
        out_specs=pl.BlockSpec((None, tb, hw), lambda b, gi, ti: (b, ti, gi)),
        out_shape=jax.ShapeDtypeStruct((bsz, t, w), BF16),
        scratch_shapes=[pltpu.VMEM((hw // LANES, LANES, LANES), F32),
                        pltpu.VMEM((8, hw), F32),
                        pltpu.VMEM((tb, hw), F32)],
        compiler_params=_params(("parallel", "parallel", "arbitrary")),
        name="rwkv7_branch",
    )(p_rkv, p_rkv, p_rkv, small, mu, mu, mu, row(w0), w2, row(a0), a2, g2,
      row(k_k), row(k_a), row(r_k), row(lnx_w), row(lnx_b))


def _ssd_kernel(xbc_ref, z_ref, dt_ref, cw_ref, cb_ref, dtb_ref, alog_ref, dsk_ref, nw_ref, exp_ref,
                o_ref, ext_ref, act_ref, st_ref):
    ch, cw = xbc_ref.shape
    width = z_ref.shape[1]
    gw = width // MB_GROUPS
    tail = 8

    @pl.when(pl.program_id(1) == 0)
    def _():
        st_ref[...] = jnp.zeros_like(st_ref)
        ext_ref[0:tail, :] = jnp.zeros((tail, cw), F32)

    @pl.when(pl.program_id(1) != 0)
    def _():
        ext_ref[0:tail, :] = ext_ref[ch:ch + tail, :]

    ext_ref[tail:tail + ch, :] = xbc_ref[...]

    cblk = 512
    for c0 in range(0, cw, cblk):
        e = ext_ref[:, c0:c0 + cblk]
        acc = cb_ref[:, c0:c0 + cblk] + cw_ref[MB_CONV - 1:MB_CONV, c0:c0 + cblk] * e[tail:]
        for s in range(1, MB_CONV):
            acc = acc + cw_ref[MB_CONV - 1 - s:MB_CONV - s, c0:c0 + cblk] * pltpu.roll(e, s, 0)[tail:]
        act_ref[:, c0:c0 + cblk] = acc * _sigmoid(acc)

    dt = _softplus(dt_ref[...] + dtb_ref[...])
    d_a = dt * (-jnp.exp(alog_ref[...])) * LOG2E
    ti = lax.broadcasted_iota(jnp.int32, (ch, ch), 0)
    tj = lax.broadcasted_iota(jnp.int32, (ch, ch), 1)
    causal = tj <= ti
    acs = _dot_split(_dot, d_a, causal.astype(BF16), False)
    acs_t = _dot_split(_dot_tn, d_a, (ti <= tj).astype(BF16), True)
    expand = exp_ref[...]
    dt_x = _dot_split(_dot, dt, expand, True)
    acs_x = _dot_split(_dot, acs, expand, True)
    acs_last_x = acs_x[ch - 1:ch, :]
    lane = lax.broadcasted_iota(jnp.int32, (ch, LANES), 1)
    m0 = lane < HEAD

    for gi in range(MB_GROUPS):
        gs = slice(gi * gw, (gi + 1) * gw)
        c_g = act_ref[:, width + MB_GROUPS * MB_STATE + gi * MB_STATE:width + MB_GROUPS * MB_STATE + (gi + 1) * MB_STATE].astype(BF16)
        b_g = act_ref[:, width + gi * MB_STATE:width + (gi + 1) * MB_STATE].astype(BF16)
        xs_g = act_ref[:, gs]
        acs_g = acs_x[:, gs]
        x_g = xs_g * dt_x[:, gs]
        cb = _dot_nt(c_g, b_g)
        s_old = st_ref[gi]
        y_off = _dot(c_g, s_old.astype(BF16)) * jnp.exp2(acs_g)
        y_diag = []
        for q in range(gw // LANES):
            x_pair = x_g[:, q * LANES:(q + 1) * LANES]
            y_pair = None
            for hh in range(2):
                head = (gi * gw) // HEAD + 2 * q + hh
                col = jnp.broadcast_to(acs[:, head:head + 1], (ch, ch))
                rowv = acs_t[head:head + 1, :]
                l_dec = jnp.exp2(jnp.where(causal, col - rowv, -jnp.inf))
                scores = (cb * l_dec).astype(BF16)
                x_m = (jnp.where(m0, x_pair, 0.0) if hh == 0 else jnp.where(m0, 0.0, x_pair)).astype(BF16)
                part = _dot(scores, x_m)
                y_pair = part if y_pair is None else y_pair + part
            y_diag.append(y_pair)
        y = jnp.concatenate(y_diag, axis=1) + y_off + xs_g * dsk_ref[:, gs]
        last_g = acs_last_x[:, gs]
        x_dec = (x_g * jnp.exp2(last_g - acs_g)).astype(BF16)
        st_ref[gi] = s_old * jnp.exp2(last_g) + _dot_tn(b_g, x_dec)
        z = z_ref[:, gs]
        yg = y * (z * _sigmoid(z))
        ms = jnp.mean(yg * yg, axis=-1, keepdims=True)
        o_ref[:, gs] = (yg * lax.rsqrt(ms + NORM_EPS) * nw_ref[:, gs]).astype(o_ref.dtype)


def _ssd_branch(p_z, p_xbc, small, conv_w, conv_b, dt_bias, a_log, d_skip, norm_w):
    bsz, t, width = p_z.shape
    cw = p_xbc.shape[2]
    heads = width // HEAD
    ch = MB_CHUNK
    pad = lambda v: jnp.zeros((1, LANES), F32).at[0, :heads].set(v.astype(F32))
    hid = jnp.arange(LANES)[:, None]
    col = jnp.arange(width)[None, :]
    expand = (hid == col // HEAD).astype(BF16)
    d_x = jnp.repeat(d_skip.astype(F32), HEAD).reshape(1, width)
    full = lambda shape: pl.BlockSpec(shape, lambda b, c: (0, 0))
    return pl.pallas_call(
        _ssd_kernel,
        grid=(bsz, t // ch),
        in_specs=[pl.BlockSpec((None, ch, cw), lambda b, c: (b, c, 0)),
                  pl.BlockSpec((None, ch, width), lambda b, c: (b, c, 0)),
                  pl.BlockSpec((None, ch, LANES), lambda b, c: (b, c, SM_DT // LANES)),
                  full((MB_CONV, cw)), full((1, cw)), full((1, LANES)), full((1, LANES)),
                  full((1, width)), full((1, width)), full((LANES, width))],
        out_specs=pl.BlockSpec((None, ch, width), lambda b, c: (b, c, 0)),
        out_shape=jax.ShapeDtypeStruct((bsz, t, width), BF16),
        scratch_shapes=[pltpu.VMEM((ch + 8, cw), F32),
                        pltpu.VMEM((ch, cw), F32),
                        pltpu.VMEM((MB_GROUPS, MB_STATE, width // MB_GROUPS), F32)],
        compiler_params=_params(("parallel", "arbitrary")),
        name="mamba2_branch",
    )(p_xbc, p_z, small, conv_w.astype(F32), conv_b.reshape(1, cw).astype(F32), pad(dt_bias), pad(a_log),
      d_x, norm_w.reshape(1, width).astype(F32), expand)


def _pad_cols(w, n):
    return jnp.zeros((w.shape[0], n), w.dtype).at[:, :w.shape[1]].set(w)


def _pad_rows(w, n):
    return jnp.zeros((n, w.shape[1]), w.dtype).at[:w.shape[0]].set(w)


def _layer(x, c_act_mod, l, norm1_g, w_in, rw_mu_rkv, rw_mu_wag, rw_w0, rw_w1, rw_w2, rw_a0, rw_a1, rw_a2, rw_g1,
           rw_g2, rw_k_k, rw_k_a, rw_r_k, rw_lnx_w, rw_lnx_b, mb_conv_w, mb_conv_b, mb_dt_bias, mb_a_log, mb_d,
           mb_norm_w, w_branch_a, w_branch_b, w_gate, b_gate, w_out, norm2_g, mlp_up, mlp_down, out_g, out_affine):
    bsz, t, d = x.shape
    m = bsz * t
    w = rw_k_k.shape[1]
    mbw = mb_norm_w.shape[1]
    cw = mb_conv_b.shape[1]
    off_rkv, off_z, off_xbc = 3 * w, 3 * w + mbw, 3 * w + mbw + cw
    sh1, sc1, gt1, sh2, sc2, gt2 = jnp.split(c_act_mod, 6, axis=-1)

    w_small = jnp.concatenate([_pad_cols(rw_w1[l], LANES), _pad_cols(rw_a1[l], LANES), rw_g1[l],
                               _pad_cols(w_in[l][:, off_xbc:], LANES)], axis=1).astype(BF16)
    h, small = _norm1(x, norm1_g[l], sc1, sh1, rw_mu_wag[l], w_small)
    h2d = h.reshape(m, d)
    w_in_t = jnp.swapaxes(w_in[l], 0, 1)
    p_rkv = _matmul(h2d, w_in_t, F32, 0, off_rkv, w_transposed=True, name="proj_rkv").reshape(bsz, t, off_rkv)
    p_z = _matmul(h2d, w_in_t, F32, off_rkv, mbw, w_transposed=True, name="proj_z").reshape(bsz, t, mbw)
    p_xbc = _matmul(h2d, w_in_t, F32, off_z, cw, w_transposed=True, name="proj_xbc").reshape(bsz, t, cw)
    gate, w_out_bf = _matmul(h2d, w_gate[l], BF16, bias=b_gate[l], act="sigmoid", side_cast=w_out[l],
                             name="proj_gate")

    o_a = _rwkv_branch(p_rkv, small, rw_mu_rkv[l], rw_w0[l], _pad_rows(rw_w2[l], LANES).astype(BF16), rw_a0[l],
                       _pad_rows(rw_a2[l], LANES).astype(BF16), rw_g2[l].astype(BF16), rw_k_k[l], rw_k_a[l],
                       rw_r_k[l], rw_lnx_w[l], rw_lnx_b[l])
    o_b = _ssd_branch(p_z, p_xbc, small, mb_conv_w[l], mb_conv_b[l], mb_dt_bias[l], mb_a_log[l], mb_d[l],
                      mb_norm_w[l])
    merged = _merge(o_a.reshape(m, w), o_b.reshape(m, mbw), w_branch_a[l], w_branch_b[l], gate)
    x1, h2 = _resid_norm(merged, w_out_bf, x.reshape(m, d), gt1, norm2_g[l], sc2, sh2,
                         rows_per_batch=t, norm_dtype=BF16, emit_x=True, name="out_proj_norm2")
    u, mlp_down_bf = _matmul(h2, mlp_up[l], BF16, act="relu2", side_cast=mlp_down[l], name="mlp_up")
    if out_affine is None:
        (y,) = _resid_norm(u, mlp_down_bf, x1, gt2, out_g, rows_per_batch=t, norm_dtype=F32,
                           emit_x=False, name="mlp_down_final_norm")
        return y.reshape(bsz, t, d)
    raise NotImplementedError("only a single layer followed by the final norm is implemented")


def kernel(x, c, ada_w, ada_b, norm1_g, w_in, rw_mu_rkv, rw_mu_wag, rw_w0, rw_w1, rw_w2, rw_a0, rw_a1, rw_a2, rw_g1, rw_g2, rw_k_k, rw_k_a, rw_r_k, rw_lnx_w, rw_lnx_b, mb_conv_w, mb_conv_b, mb_dt_bias, mb_a_log, mb_d, mb_norm_w, w_branch_a, w_branch_b, w_gate, b_gate, w_out, norm2_g, mlp_up, mlp_down, final_norm_g):
    depth = ada_w.shape[0]
    assert depth == 1, "the fused final-norm epilogue assumes a single layer"
    mod = _ada_mod(c, ada_w[0], ada_b[0])
    return _layer(x, mod, 0, norm1_g, w_in, rw_mu_rkv, rw_mu_wag, rw_w0, rw_w1, rw_w2, rw_a0, rw_a1, rw_a2, rw_g1,
                  rw_g2, rw_k_k, rw_k_a, rw_r_k, rw_lnx_w, rw_lnx_b, mb_conv_w, mb_conv_b, mb_dt_bias, mb_a_log,
                  mb_d, mb_norm_w, w_branch_a, w_branch_b, w_gate, b_gate, w_out, norm2_g, mlp_up, mlp_down,
                  final_norm_g, None)
```

```python
import functools
import math

import jax
import jax.numpy as jnp
from jax import lax
from jax.experimental import pallas as pl
from jax.experimental.pallas import tpu as pltpu

F32 = jnp.float32
BF16 = jnp.bfloat16

LANES = 128
HEAD = 64
RW_CHUNK = 64
RW_DECAY_LORA = 96
RW_AAA_LORA = 96
RW_GATE_LORA = 256
RW_LNX_EPS = 64e-5
MB_GROUPS = 8
MB_STATE = 128
MB_CONV = 4
MB_CHUNK = 128
NORM_EPS = 1e-5
LOG2E = 1.4426950408889634
VMEM_LIMIT = 56 * 1024 * 1024

SM_W, SM_A, SM_G, SM_DT, SM_WIDTH = 0, 128, 256, 512, 640


def _params(sem):
    return pltpu.CompilerParams(dimension_semantics=sem, vmem_limit_bytes=VMEM_LIMIT)


def _sigmoid(x):
    return 1.0 / (1.0 + jnp.exp(-x))


def _softplus(x):
    return jnp.maximum(x, 0.0) + jnp.log1p(jnp.exp(-jnp.abs(x)))


def _dot(a, b):
    return jnp.dot(a, b, preferred_element_type=F32)


def _dot_nt(a, b):
    return lax.dot_general(a, b, (((1,), (1,)), ((), ())), preferred_element_type=F32)


def _dot_tn(a, b):
    return lax.dot_general(a, b, (((0,), (0,)), ((), ())), preferred_element_type=F32)


def _split_bf16(x, terms):
    parts = []
    for _ in range(terms - 1):
        hi = x.astype(BF16)
        parts.append(hi)
        x = x - hi.astype(F32)
    parts.append(x.astype(BF16))
    return parts


def _dot_split(dot_fn, x, m, x_is_lhs, terms=3):
    parts = _split_bf16(x, terms)
    prods = [dot_fn(p, m) if x_is_lhs else dot_fn(m, p) for p in parts]
    out = prods[0]
    for p in prods[1:]:
        out = out + p
    return out


def _ada_kernel(c_ref, w_ref, b_ref, o_ref):
    c = c_ref[...]
    ca = (c * _sigmoid(c)).astype(BF16)
    o_ref[...] = _dot(ca, w_ref[...].astype(BF16)) + b_ref[...]


def _ada_mod(c, ada_w, ada_b):
    bsz, d = c.shape
    n = ada_w.shape[1]
    rows = 8
    c_pad = jnp.zeros((rows, d), F32).at[:bsz].set(c)
    tn = 1024
    out = pl.pallas_call(
        _ada_kernel,
        grid=(n // tn,),
        in_specs=[pl.BlockSpec((rows, d), lambda j: (0, 0)),
                  pl.BlockSpec((d, tn), lambda j: (0, j)),
                  pl.BlockSpec((1, tn), lambda j: (0, j))],
        out_specs=pl.BlockSpec((rows, tn), lambda j: (0, j)),
        out_shape=jax.ShapeDtypeStruct((rows, n), F32),
        compiler_params=_params(("parallel",)),
        name="ada_mod",
    )(c_pad, ada_w, ada_b.reshape(1, n))
    return out[:bsz]


def _small_proj_weights(mu_wag, w1, a1, g1, w_dt):
    lora = [_pad_cols(w1, LANES), _pad_cols(a1, LANES), g1]
    direct = [(1.0 - mu_wag[i])[:, None] * w for i, w in enumerate(lora)]
    shifted = [mu_wag[i][:, None] * w for i, w in enumerate(lora)]
    return jnp.concatenate(direct + [_pad_cols(w_dt, LANES)] + shifted, axis=1).astype(BF16)


def _norm1_kernel(x_ref, g_ref, sc_ref, sh_ref, ws_ref, h_ref, sm_ref, prev_ref):
    i = pl.program_id(1)

    @pl.when(i == 0)
    def _():
        prev_ref[...] = jnp.zeros_like(prev_ref)

    x = x_ref[...]
    tm = x.shape[0]
    ms = jnp.mean(x * x, axis=-1, keepdims=True)
    h = x * lax.rsqrt(ms + NORM_EPS) * g_ref[...] * (1.0 + sc_ref[...]) + sh_ref[...]
    hb = h.astype(BF16)
    h_ref[...] = hb
    p = _dot(hb, ws_ref[...])
    q = p[:, SM_WIDTH:]
    qs = pltpu.roll(q, 1, 0)
    row = lax.broadcasted_iota(jnp.int32, (8, SM_DT), 0)
    qs = jnp.concatenate([jnp.where(row == 0, prev_ref[0:1, :], qs[0:8]), qs[8:]], axis=0)
    prev_ref[0:1, :] = q[tm - 1:tm, :]
    sm_ref[:, :SM_DT] = p[:, :SM_DT] + qs
    sm_ref[:, SM_DT:] = p[:, SM_DT:SM_WIDTH]


def _norm1(x, g, sc, sh, w_small, tm=256):
    bsz, t, d = x.shape
    return pl.pallas_call(
        _norm1_kernel,
        grid=(bsz, t // tm),
        in_specs=[pl.BlockSpec((None, tm, d), lambda b, i: (b, i, 0)),
                  pl.BlockSpec((1, d), lambda b, i: (0, 0)),
                  pl.BlockSpec((None, 1, d), lambda b, i: (b, 0, 0)),
                  pl.BlockSpec((None, 1, d), lambda b, i: (b, 0, 0)),
                  pl.BlockSpec((d, SM_WIDTH + SM_DT), lambda b, i: (0, 0))],
        out_specs=[pl.BlockSpec((None, tm, d), lambda b, i: (b, i, 0)),
                   pl.BlockSpec((None, tm, SM_WIDTH), lambda b, i: (b, i, 0))],
        out_shape=[jax.ShapeDtypeStruct((bsz, t, d), BF16),
                   jax.ShapeDtypeStruct((bsz, t, SM_WIDTH), F32)],
        scratch_shapes=[pltpu.VMEM((8, SM_DT), F32)],
        compiler_params=_params(("parallel", "arbitrary")),
        name="norm1_small_proj",
    )(x, g.reshape(1, d), sc.reshape(bsz, 1, d), sh.reshape(bsz, 1, d), w_small)


def _mm_kernel(a_ref, w_ref, *rest, act, has_bias, has_side, w_transposed):
    rest = list(rest)
    bias_ref = rest.pop(0) if has_bias else None
    side_ref = rest.pop(0) if has_side else None
    o_ref = rest.pop(0)
    side_o_ref = rest.pop(0) if has_side else None
    (wb_ref,) = rest

    @pl.when(pl.program_id(1) == 0)
    def _():
        w = w_ref[...]
        wb_ref[...] = (w.T if w_transposed else w).astype(BF16)

    if has_side:
        side_o_ref[...] = side_ref[...].astype(BF16)

    y = _dot(a_ref[...], wb_ref[...])
    if has_bias:
        y = y + bias_ref[...]
    if act == "sigmoid":
        y = _sigmoid(y)
    elif act == "relu2":
        y = jnp.maximum(y, 0.0)
        y = y * y
    o_ref[...] = y.astype(o_ref.dtype)


def _matmul(a, w, out_dtype, col0=0, n=None, bias=None, act=None, w_transposed=False, side_cast=None, tm=1024,
            tn=1024, name="matmul"):
    m, k = a.shape
    n = w.shape[0 if w_transposed else 1] if n is None else n
    tm, tn = min(tm, m), min(tn, n)
    assert col0 % tn == 0 and n % tn == 0 and m % tm == 0
    jb = col0 // tn
    w_spec = (pl.BlockSpec((tn, k), lambda j, i: (j + jb, 0)) if w_transposed
              else pl.BlockSpec((k, tn), lambda j, i: (0, j + jb)))
    in_specs = [pl.BlockSpec((tm, k), lambda j, i: (i, 0)), w_spec]
    args = [a, w]
    if bias is not None:
        in_specs.append(pl.BlockSpec((1, tn), lambda j, i: (0, j)))
        args.append(bias.reshape(1, n).astype(F32))
    out_specs = [pl.BlockSpec((tm, tn), lambda j, i: (i, j))]
    out_shape = [jax.ShapeDtypeStruct((m, n), out_dtype)]
    if side_cast is not None:
        ni = m // tm
        rows = side_cast.shape[0] // ((n // tn) * ni)
        assert rows * (n // tn) * ni == side_cast.shape[0] and rows % 16 == 0
        side_spec = pl.BlockSpec((rows, side_cast.shape[1]), lambda j, i: (j * ni + i, 0))
        in_specs.append(side_spec)
        args.append(side_cast)
        out_specs.append(side_spec)
        out_shape.append(jax.ShapeDtypeStruct(side_cast.shape, BF16))
    outs = pl.pallas_call(
        functools.partial(_mm_kernel, act=act, has_bias=bias is not None, has_side=side_cast is not None,
                          w_transposed=w_transposed),
        grid=(n // tn, m // tm),
        in_specs=in_specs,
        out_specs=out_specs,
        out_shape=out_shape,
        scratch_shapes=[pltpu.VMEM((k, tn), BF16)],
        compiler_params=_params(("parallel", "arbitrary")),
        name=name,
    )(*args)
    return outs if side_cast is not None else outs[0]


def _merge_kernel(oa_ref, ob_ref, wa_ref, wb_ref, ga_ref, gb_ref, side_ref, o_ref, side_o_ref):
    ya = _dot(oa_ref[...], wa_ref[...])
    yb = _dot(ob_ref[...], wb_ref[...])
    o_ref[...] = (ga_ref[...].astype(F32) * ya + gb_ref[...].astype(F32) * yb).astype(o_ref.dtype)
    side_o_ref[...] = side_ref[...].astype(BF16)


def _merge(o_a, o_b, wa, wb, gate, side_cast, tm=512, tn=1024):
    m, ka = o_a.shape
    kb = o_b.shape[1]
    n = wa.shape[1]
    off = n // tn
    ni = m // tm
    rows = side_cast.shape[0] // (off * ni)
    assert rows * off * ni == side_cast.shape[0] and rows % 16 == 0
    side_spec = pl.BlockSpec((rows, side_cast.shape[1]), lambda j, i: (j * ni + i, 0))
    return pl.pallas_call(
        _merge_kernel,
        grid=(n // tn, m // tm),
        in_specs=[pl.BlockSpec((tm, ka), lambda j, i: (i, 0)),
                  pl.BlockSpec((tm, kb), lambda j, i: (i, 0)),
                  pl.BlockSpec((ka, tn), lambda j, i: (0, j)),
                  pl.BlockSpec((kb, tn), lambda j, i: (0, j)),
                  pl.BlockSpec((tm, tn), lambda j, i: (i, j)),
                  pl.BlockSpec((tm, tn), lambda j, i: (i, j + off)),
                  side_spec],
        out_specs=[pl.BlockSpec((tm, tn), lambda j, i: (i, j)), side_spec],
        out_shape=[jax.ShapeDtypeStruct((m, n), BF16), jax.ShapeDtypeStruct(side_cast.shape, BF16)],
        compiler_params=_params(("parallel", "parallel")),
        name="branch_merge",
    )(o_a, o_b, wa, wb, gate, gate, side_cast)


def _resid_norm_kernel(a_ref, b_ref, x_ref, gt_ref, g_ref, *rest, affine, emit_x):
    rest = list(rest)
    if affine:
        sc_ref, sh_ref = rest[0], rest[1]
        rest = rest[2:]
    if emit_x:
        xo_ref, no_ref, acc_ref = rest
    else:
        no_ref, acc_ref = rest
    kk = pl.program_id(1)

    @pl.when(kk == 0)
    def _():
        acc_ref[...] = jnp.zeros_like(acc_ref)

    acc_ref[...] += _dot(a_ref[...], b_ref[...])

    @pl.when(kk == pl.num_programs(1) - 1)
    def _():
        xn = x_ref[...] + gt_ref[...] * acc_ref[...]
        if emit_x:
            xo_ref[...] = xn
        ms = jnp.mean(xn * xn, axis=-1, keepdims=True)
        y = xn * lax.rsqrt(ms + NORM_EPS) * g_ref[...]
        if affine:
            y = y * (1.0 + sc_ref[...]) + sh_ref[...]
        no_ref[...] = y.astype(no_ref.dtype)


def _resid_norm(a, b, x, gt, g, sc=None, sh=None, *, rows_per_batch, norm_dtype, emit_x, tm=512, tk=2048,
                name="resid_norm"):
    m, k = a.shape
    n = b.shape[1]
    bsz = gt.shape[0]
    tpb = rows_per_batch // tm
    affine = sc is not None
    vec = lambda v: v.reshape(bsz, 1, n)
    in_specs = [pl.BlockSpec((tm, tk), lambda i, kk: (i, kk)),
                pl.BlockSpec((tk, n), lambda i, kk: (kk, 0)),
                pl.BlockSpec((tm, n), lambda i, kk: (i, 0)),
                pl.BlockSpec((None, 1, n), lambda i, kk: (i // tpb, 0, 0)),
                pl.BlockSpec((1, n), lambda i, kk: (0, 0))]
    args = [a, b, x, vec(gt), g.reshape(1, n)]
    if affine:
        in_specs += [pl.BlockSpec((None, 1, n), lambda i, kk: (i // tpb, 0, 0))] * 2
        args += [vec(sc), vec(sh)]
    out_specs = [pl.BlockSpec((tm, n), lambda i, kk: (i, 0))]
    out_shape = [jax.ShapeDtypeStruct((m, n), norm_dtype)]
    if emit_x:
        out_specs = [pl.BlockSpec((tm, n), lambda i, kk: (i, 0))] + out_specs
        out_shape = [jax.ShapeDtypeStruct((m, n), F32)] + out_shape
    return pl.pallas_call(
        functools.partial(_resid_norm_kernel, affine=affine, emit_x=emit_x),
        grid=(m // tm, k // tk),
        in_specs=in_specs,
        out_specs=out_specs,
        out_shape=out_shape,
        scratch_shapes=[pltpu.VMEM((tm, n), F32)],
        compiler_params=_params(("parallel", "arbitrary")),
        name=name,
    )(*args)


def _seg_sum(x, ones_bd, terms):
    cols = x.shape[1]
    wid = ones_bd.shape[0]
    outs = [_dot_split(_dot, x[:, c:c + wid], ones_bd, True, terms=terms) for c in range(0, cols, wid)]
    return outs[0] if len(outs) == 1 else jnp.concatenate(outs, axis=1)


def _inv_unit_lower_all(a_list, ri, ci):
    n = a_list[0].shape[0]
    eye = (ri == ci).astype(F32)
    same16 = (ri // 16) == (ci // 16)
    same32 = (ri // 32) == (ci // 32)
    t = [eye + jnp.where(same16, a, 0.0) for a in a_list]
    p = [jnp.where(same16, a, 0.0).astype(BF16) for a in a_list]
    p = [_dot(x, x).astype(BF16) for x in p]
    for _ in range(2):
        r = [_dot(x, jnp.concatenate([tt.astype(BF16), x], axis=1)) for x, tt in zip(p, t)]
        t = [tt + rr[:, :n] for tt, rr in zip(t, r)]
        p = [rr[:, n:].astype(BF16) for rr in r]
    t = [tt + _dot(x, tt.astype(BF16)) for x, tt in zip(p, t)]
    for half, sel in ((16, same32 & (~same16)), (32, ~same32)):
        lo = [s for s in range(0, n, half) if (s // half) % 2 == 1]
        tb = [tt.astype(BF16) for tt in t]
        t_lo = [jnp.concatenate([tt[s:s + half] for s in lo], axis=0).astype(BF16) for tt in t]
        x = [_dot(tl, jnp.where(sel, a, 0.0).astype(BF16)).astype(BF16) for tl, a in zip(t_lo, a_list)]
        upd = [_dot(xx, b) for xx, b in zip(x, tb)]

        def add_rows(tt, u):
            parts = [tt[s:s + half] for s in range(0, n, half)]
            for j, s in enumerate(lo):
                parts[s // half] = parts[s // half] + u[j * half:(j + 1) * half]
            return jnp.concatenate(parts, axis=0)

        t = [add_rows(tt, u) for tt, u in zip(t, upd)]
    return t


def _rwkv_kernel(pr_ref, pk_ref, pv_ref, sm_ref, mur_ref, muk_ref, muv_ref, w0_ref, w2_ref, a0_ref, a2_ref,
                 g2_ref, kk_ref, ka_ref, rk_ref, lnw_ref, lnb_ref, o_ref, s_ref, prev_ref, y_ref):
    tb, hw = pr_ref.shape
    n_chunk = tb // RW_CHUNK
    n_pair = hw // LANES
    c_len = RW_CHUNK

    @pl.when(pl.program_id(2) == 0)
    def _():
        s_ref[...] = jnp.zeros_like(s_ref)
        prev_ref[...] = jnp.zeros_like(prev_ref)

    row = lax.broadcasted_iota(jnp.int32, (8, hw), 0)

    def shifted_mix(p_ref, mu_ref, slot):
        p = p_ref[...]
        ps = pltpu.roll(p, 1, 0)
        ps = jnp.concatenate([jnp.where(row == 0, prev_ref[slot:slot + 1, :], ps[0:8]), ps[8:]], axis=0)
        prev_ref[slot:slot + 1, :] = p[tb - 1:tb, :]
        return p + (ps - p) * mu_ref[...]

    r = shifted_mix(pr_ref, mur_ref, 0)
    k = shifted_mix(pk_ref, muk_ref, 1)
    v = shifted_mix(pv_ref, muv_ref, 2)

    lw = jnp.tanh(sm_ref[:, SM_W:SM_A]).astype(BF16)
    logw = (-math.exp(-0.5) * LOG2E) * _sigmoid(w0_ref[...] + _dot(lw, w2_ref[...]))
    a = _sigmoid(a0_ref[...] + _dot(sm_ref[:, SM_A:SM_G].astype(BF16), a2_ref[...]))
    g = _dot(_sigmoid(sm_ref[:, SM_G:SM_DT]).astype(BF16), g2_ref[...])

    li = lax.broadcasted_iota(jnp.int32, (LANES, LANES), 0)
    lj = lax.broadcasted_iota(jnp.int32, (LANES, LANES), 1)
    same_head = (li // HEAD) == (lj // HEAD)
    seg_w = min(2 * LANES, hw)
    si = lax.broadcasted_iota(jnp.int32, (seg_w, seg_w), 0)
    sj = lax.broadcasted_iota(jnp.int32, (seg_w, seg_w), 1)
    ones_bd = ((si // HEAD) == (sj // HEAD)).astype(BF16)

    kkv = k * kk_ref[...]
    kkn = kkv * lax.rsqrt(jnp.maximum(_seg_sum(kkv * kkv, ones_bd, 2), 1e-24))
    k2 = k * (1.0 + (a - 1.0) * ka_ref[...])
    aa = -kkn
    bb = kkn * a
    bonus = _seg_sum(r * k2 * rk_ref[...], ones_bd, 1) * v

    ti = lax.broadcasted_iota(jnp.int32, (tb, tb), 0)
    tj = lax.broadcasted_iota(jnp.int32, (tb, tb), 1)
    cum_mat = (((ti // c_len) == (tj // c_len)) & (tj <= ti)).astype(BF16)
    lc = _dot_split(_dot, logw, cum_mat, False, terms=2)
    e_pos = jnp.exp2(lc)
    e_neg = jnp.exp2(-lc)
    rt = r * e_pos
    at = aa * jnp.exp2(lc - logw)
    bt = bb * e_neg
    kt = k2 * e_neg

    lane = lax.broadcasted_iota(jnp.int32, (c_len, LANES), 1)
    m0 = lane < HEAD

    def bd(x):
        xb = x.astype(BF16)
        zero = jnp.zeros_like(xb)
        return jnp.concatenate([jnp.where(m0, xb, zero), jnp.where(m0, zero, xb)], axis=0)

    strict = same_head & ((li % HEAD) > (lj % HEAD))
    incl = same_head & ((li % HEAD) >= (lj % HEAD))

    inst = [(c, p) for c in range(n_chunk) for p in range(n_pair)]
    rows = lambda c: slice(c * c_len, (c + 1) * c_len)
    lanes = lambda p: slice(p * LANES, (p + 1) * LANES)
    tile = lambda x: [bd(x[rows(c), lanes(p)]) for c, p in inst]
    a_bd, r_bd, b_bd, k_bd, v_bd = tile(at), tile(rt), tile(bt), tile(kt), tile(v)

    bk_bd = [jnp.concatenate([z, w], axis=0) for z, w in zip(b_bd, k_bd)]
    gm = [_dot_nt(jnp.concatenate([x, y], axis=0), zw) for x, y, zw in zip(a_bd, r_bd, bk_bd)]
    a_ab = [jnp.where(strict, x[0:LANES, 0:LANES], 0.0) for x in gm]
    a_ak = [jnp.where(strict, x[0:LANES, LANES:], 0.0).astype(BF16) for x in gm]
    incl2 = jnp.concatenate([incl, incl], axis=1)
    a_rbk = [jnp.where(incl2, x[LANES:, :], 0.0).astype(BF16) for x in gm]
    t_inv = [x.astype(BF16) for x in _inv_unit_lower_all(a_ab, li, lj)]
    av = [_dot(x, y).astype(BF16) for x, y in zip(a_ak, v_bd)]
    tav = [_dot(t, jnp.concatenate([x, y], axis=1)) for t, x, y in zip(t_inv, a_bd, av)]
    ta = [x[:, :LANES].astype(BF16) for x in tav]
    uv = [x[:, LANES:] for x in tav]

    state = [s_ref[p] for p in range(n_pair)]
    for c in range(n_chunk):
        ids = [c * n_pair + p for p in range(n_pair)]
        s_b = [s.astype(BF16) for s in state]
        u_bd = [(_dot_nt(ta[i], s_b[p]) + uv[i]).astype(BF16) for p, i in enumerate(ids)]
        g_end = [e_pos[(c + 1) * c_len - 1:(c + 1) * c_len, lanes(p)] for p in range(n_pair)]
        uv_bd = [jnp.concatenate([u_bd[p], v_bd[i]], axis=0) for p, i in enumerate(ids)]
        state = [g_end[p] * (state[p] + _dot_tn(uv_bd[p], bk_bd[i])) for p, i in enumerate(ids)]
        for p, i in enumerate(ids):
            y_bd = _dot_nt(r_bd[i], s_b[p]) + _dot(a_rbk[i], uv_bd[p])
            y_ref[rows(c), lanes(p)] = y_bd[0:c_len] + y_bd[c_len:]
    for p in range(n_pair):
        s_ref[p] = state[p]

    y = y_ref[...]
    mu = _seg_sum(y, ones_bd, 1) * (1.0 / HEAD)
    d = y - mu
    var = _seg_sum(d * d, ones_bd, 1) * (1.0 / HEAD)
    yn = d * lax.rsqrt(var + RW_LNX_EPS) * lnw_ref[...] + lnb_ref[...]
    o_ref[...] = ((yn + bonus) * g).astype(o_ref.dtype)


def _rwkv_branch(p_rkv, small, mu_rkv, w0, w2, a0, a2, g2, k_k, k_a, r_k, lnx_w, lnx_b, tb=256, hw=1024):
    bsz, t, w3 = p_rkv.shape
    w = w3 // 3
    nb = w // hw
    row = lambda v: v.reshape(1, -1).astype(F32)
    vec_spec = lambda off: pl.BlockSpec((1, hw), lambda b, gi, ti: (0, gi + off))
    rkv_spec = lambda off: pl.BlockSpec((None, tb, hw), lambda b, gi, ti: (b, ti, gi + off))
    mat_spec = lambda rows: pl.BlockSpec((rows, hw), lambda b, gi, ti: (0, gi))
    mu = row(mu_rkv)
    return pl.pallas_call(
        _rwkv_kernel,
        grid=(bsz, nb, t // tb),
        in_specs=[rkv_spec(0), rkv_spec(nb), rkv_spec(2 * nb),
                  pl.BlockSpec((None, tb, SM_WIDTH), lambda b, gi, ti: (b, ti, 0)),
                  vec_spec(0), vec_spec(nb), vec_spec(2 * nb),
                  vec_spec(0), mat_spec(LANES), vec_spec(0), mat_spec(LANES), mat_spec(RW_GATE_LORA),
                  vec_spec(0), vec_spec(0), vec_spec(0), vec_spec(0), vec_spec(0)],
        out_specs=pl.BlockSpec((None, tb, hw), lambda b, gi, ti: (b, ti, gi)),
        out_shape=jax.ShapeDtypeStruct((bsz, t, w), BF16),
        scratch_shapes=[pltpu.VMEM((hw // LANES, LANES, LANES), F32),
                        pltpu.VMEM((8, hw), F32),
                        pltpu.VMEM((tb, hw), F32)],
        compiler_params=_params(("parallel", "parallel", "arbitrary")),
        name="rwkv7_branch",
    )(p_rkv, p_rkv, p_rkv, small, mu, mu, mu, row(w0), w2, row(a0), a2, g2,
      row(k_k), row(k_a), row(r_k), row(lnx_w), row(lnx_b))


def _ssd_kernel(xbc_ref, z_ref, dt_ref, cw_ref, cb_ref, dtb_ref, alog_ref, dsk_ref, nw_ref, exp_ref,
                o_ref, ext_ref, act_ref, st_ref):
    ch, cw = xbc_ref.shape
    width = z_ref.shape[1]
    gw = width // MB_GROUPS
    tail = 8

    @pl.when(pl.program_id(1) == 0)
    def _():
        st_ref[...] = jnp.zeros_like(st_ref)
        ext_ref[0:tail, :] = jnp.zeros((tail, cw), F32)

    @pl.when(pl.program_id(1) != 0)
    def _():
        ext_ref[0:tail, :] = ext_ref[ch:ch + tail, :]

    ext_ref[tail:tail + ch, :] = xbc_ref[...]

    cblk = 512
    for c0 in range(0, cw, cblk):
        e = ext_ref[:, c0:c0 + cblk]
        acc = cb_ref[:, c0:c0 + cblk] + cw_ref[MB_CONV - 1:MB_CONV, c0:c0 + cblk] * e[tail:]
        for s in range(1, MB_CONV):
            acc = acc + cw_ref[MB_CONV - 1 - s:MB_CONV - s, c0:c0 + cblk] * pltpu.roll(e, s, 0)[tail:]
        act_ref[:, c0:c0 + cblk] = acc * _sigmoid(acc)

    dt = _softplus(dt_ref[...] + dtb_ref[...])
    d_a = dt * (-jnp.exp(alog_ref[...])) * LOG2E
    ti = lax.broadcasted_iota(jnp.int32, (ch, ch), 0)
    tj = lax.broadcasted_iota(jnp.int32, (ch, ch), 1)
    causal = tj <= ti
    acs = _dot_split(_dot, d_a, causal.astype(BF16), False)
    acs_t = _dot_split(_dot_tn, d_a, (ti <= tj).astype(BF16), True)
    expand = exp_ref[...]
    dt_x = _dot_split(_dot, dt, expand, True)
    acs_x = _dot_split(_dot, acs, expand, True)
    acs_last_x = acs_x[ch - 1:ch, :]
    lane = lax.broadcasted_iota(jnp.int32, (ch, LANES), 1)
    m0 = lane < HEAD

    for gi in range(MB_GROUPS):
        gs = slice(gi * gw, (gi + 1) * gw)
        c_g = act_ref[:, width + MB_GROUPS * MB_STATE + gi * MB_STATE:width + MB_GROUPS * MB_STATE + (gi + 1) * MB_STATE].astype(BF16)
        b_g = act_ref[:, width + gi * MB_STATE:width + (gi + 1) * MB_STATE].astype(BF16)
        xs_g = act_ref[:, gs]
        acs_g = acs_x[:, gs]
        x_g = xs_g * dt_x[:, gs]
        cb = _dot_nt(c_g, b_g)
        s_old = st_ref[gi]
        y_off = _dot(c_g, s_old.astype(BF16)) * jnp.exp2(acs_g)
        y_diag = []
        for q in range(gw // LANES):
            x_pair = x_g[:, q * LANES:(q + 1) * LANES]
            y_pair = None
            for hh in range(2):
                head = (gi * gw) // HEAD + 2 * q + hh
                col = jnp.broadcast_to(acs[:, head:head + 1], (ch, ch))
                rowv = acs_t[head:head + 1, :]
                l_dec = jnp.exp2(jnp.where(causal, col - rowv, -jnp.inf))
                scores = (cb * l_dec).astype(BF16)
                x_m = (jnp.where(m0, x_pair, 0.0) if hh == 0 else jnp.where(m0, 0.0, x_pair)).astype(BF16)
                part = _dot(scores, x_m)
                y_pair = part if y_pair is None else y_pair + part
            y_diag.append(y_pair)
        y = jnp.concatenate(y_diag, axis=1) + y_off + xs_g * dsk_ref[:, gs]
        last_g = acs_last_x[:, gs]
        x_dec = (x_g * jnp.exp2(last_g - acs_g)).astype(BF16)
        st_ref[gi] = s_old * jnp.exp2(last_g) + _dot_tn(b_g, x_dec)
        z = z_ref[:, gs]
        yg = y * (z * _sigmoid(z))
        ms = jnp.mean(yg * yg, axis=-1, keepdims=True)
        o_ref[:, gs] = (yg * lax.rsqrt(ms + NORM_EPS) * nw_ref[:, gs]).astype(o_ref.dtype)


def _ssd_branch(p_z, p_xbc, small, conv_w, conv_b, dt_bias, a_log, d_skip, norm_w):
    bsz, t, width = p_z.shape
    cw = p_xbc.shape[2]
    heads = width // HEAD
    ch = MB_CHUNK
    pad = lambda v: jnp.zeros((1, LANES), F32).at[0, :heads].set(v.astype(F32))
    hid = jnp.arange(LANES)[:, None]
    col = jnp.arange(width)[None, :]
    expand = (hid == col // HEAD).astype(BF16)
    d_x = jnp.repeat(d_skip.astype(F32), HEAD).reshape(1, width)
    full = lambda shape: pl.BlockSpec(shape, lambda b, c: (0, 0))
    return pl.pallas_call(
        _ssd_kernel,
        grid=(bsz, t // ch),
        in_specs=[pl.BlockSpec((None, ch, cw), lambda b, c: (b, c, 0)),
                  pl.BlockSpec((None, ch, width), lambda b, c: (b, c, 0)),
                  pl.BlockSpec((None, ch, LANES), lambda b, c: (b, c, SM_DT // LANES)),
                  full((MB_CONV, cw)), full((1, cw)), full((1, LANES)), full((1, LANES)),
                  full((1, width)), full((1, width)), full((LANES, width))],
        out_specs=pl.BlockSpec((None, ch, width), lambda b, c: (b, c, 0)),
        out_shape=jax.ShapeDtypeStruct((bsz, t, width), BF16),
        scratch_shapes=[pltpu.VMEM((ch + 8, cw), F32),
                        pltpu.VMEM((ch, cw), F32),
                        pltpu.VMEM((MB_GROUPS, MB_STATE, width // MB_GROUPS), F32)],
        compiler_params=_params(("parallel", "arbitrary")),
        name="mamba2_branch",
    )(p_xbc, p_z, small, conv_w.astype(F32), conv_b.reshape(1, cw).astype(F32), pad(dt_bias), pad(a_log),
      d_x, norm_w.reshape(1, width).astype(F32), expand)


def _pad_cols(w, n):
    return jnp.zeros((w.shape[0], n), w.dtype).at[:, :w.shape[1]].set(w)


def _pad_rows(w, n):
    return jnp.zeros((n, w.shape[1]), w.dtype).at[:w.shape[0]].set(w)


def _layer(x, c_act_mod, l, norm1_g, w_in, rw_mu_rkv, rw_mu_wag, rw_w0, rw_w1, rw_w2, rw_a0, rw_a1, rw_a2, rw_g1,
           rw_g2, rw_k_k, rw_k_a, rw_r_k, rw_lnx_w, rw_lnx_b, mb_conv_w, mb_conv_b, mb_dt_bias, mb_a_log, mb_d,
           mb_norm_w, w_branch_a, w_branch_b, w_gate, b_gate, w_out, norm2_g, mlp_up, mlp_down, out_g, out_affine):
    bsz, t, d = x.shape
    m = bsz * t
    w = rw_k_k.shape[1]
    mbw = mb_norm_w.shape[1]
    cw = mb_conv_b.shape[1]
    off_rkv, off_z, off_xbc = 3 * w, 3 * w + mbw, 3 * w + mbw + cw
    sh1, sc1, gt1, sh2, sc2, gt2 = jnp.split(c_act_mod, 6, axis=-1)

    w_small = _small_proj_weights(rw_mu_wag[l], rw_w1[l], rw_a1[l], rw_g1[l], w_in[l][:, off_xbc:])
    h, small = _norm1(x, norm1_g[l], sc1, sh1, w_small)
    h2d = h.reshape(m, d)
    w_in_t = jnp.swapaxes(w_in[l], 0, 1)
    p_rkv = _matmul(h2d, w_in_t, F32, 0, off_rkv, w_transposed=True, name="proj_rkv").reshape(bsz, t, off_rkv)
    p_z, wb_bf = _matmul(h2d, w_in_t, F32, off_rkv, mbw, w_transposed=True, side_cast=w_branch_b[l], name="proj_z")
    p_z = p_z.reshape(bsz, t, mbw)
    p_xbc = _matmul(h2d, w_in_t, F32, off_z, cw, w_transposed=True, name="proj_xbc").reshape(bsz, t, cw)
    gate, wa_bf = _matmul(h2d, w_gate[l], BF16, bias=b_gate[l], act="sigmoid", side_cast=w_branch_a[l],
                          name="proj_gate")

    o_a = _rwkv_branch(p_rkv, small, rw_mu_rkv[l], rw_w0[l], _pad_rows(rw_w2[l], LANES).astype(BF16), rw_a0[l],
                       _pad_rows(rw_a2[l], LANES).astype(BF16), rw_g2[l].astype(BF16), rw_k_k[l], rw_k_a[l],
                       rw_r_k[l], rw_lnx_w[l], rw_lnx_b[l])
    o_b = _ssd_branch(p_z, p_xbc, small, mb_conv_w[l], mb_conv_b[l], mb_dt_bias[l], mb_a_log[l], mb_d[l],
                      mb_norm_w[l])
    merged, w_out_bf = _merge(o_a.reshape(m, w), o_b.reshape(m, mbw), wa_bf, wb_bf, gate, w_out[l])
    x1, h2 = _resid_norm(merged, w_out_bf, x.reshape(m, d), gt1, norm2_g[l], sc2, sh2,
                         rows_per_batch=t, norm_dtype=BF16, emit_x=True, name="out_proj_norm2")
    u, mlp_down_bf = _matmul(h2, mlp_up[l], BF16, act="relu2", side_cast=mlp_down[l], name="mlp_up")
    if out_affine is None:
        (y,) = _resid_norm(u, mlp_down_bf, x1, gt2, out_g, rows_per_batch=t, norm_dtype=F32,
                           emit_x=False, name="mlp_down_final_norm")
        return y.reshape(bsz, t, d)
    raise NotImplementedError("only a single layer followed by the final norm is implemented")


def kernel(x, c, ada_w, ada_b, norm1_g, w_in, rw_mu_rkv, rw_mu_wag, rw_w0, rw_w1, rw_w2, rw_a0, rw_a1, rw_a2, rw_g1, rw_g2, rw_k_k, rw_k_a, rw_r_k, rw_lnx_w, rw_lnx_b, mb_conv_w, mb_conv_b, mb_dt_bias, mb_a_log, mb_d, mb_norm_w, w_branch_a, w_branch_b, w_gate, b_gate, w_out, norm2_g, mlp_up, mlp_down, final_norm_g):
    depth = ada_w.shape[0]
    assert depth == 1, "the fused final-norm epilogue assumes a single layer"
    mod = _ada_mod(c, ada_w[0], ada_b[0])
    return _layer(x, mod, 0, norm1_g, w_in, rw_mu_rkv, rw_mu_wag, rw_w0, rw_w1, rw_w2, rw_a0, rw_a1, rw_a2, rw_g1,
                  rw_g2, rw_k_k, rw_k_a, rw_r_k, rw_lnx_w, rw_lnx_b, mb_conv_w, mb_conv_b, mb_dt_bias, mb_a_log,
                  mb_d, mb_norm_w, w_branch_a, w_branch_b, w_gate, b_gate, w_out, norm2_g, mlp_up, mlp_down,
                  final_norm_g, None)
```

```python
import functools
import math

import jax
import jax.numpy as jnp
from jax import lax
from jax.experimental import pallas as pl
from jax.experimental.pallas import tpu as pltpu

F32 = jnp.float32
BF16 = jnp.bfloat16

LANES = 128
HEAD = 64
RW_CHUNK = 64
RW_DECAY_LORA = 96
RW_AAA_LORA = 96
RW_GATE_LORA = 256
RW_LNX_EPS = 64e-5
MB_GROUPS = 8
MB_STATE = 128
MB_CONV = 4
MB_CHUNK = 128
NORM_EPS = 1e-5
LOG2E = 1.4426950408889634
VMEM_LIMIT = 56 * 1024 * 1024

SM_W, SM_A, SM_G, SM_DT, SM_WIDTH = 0, 128, 256, 512, 640


def _params(sem):
    return pltpu.CompilerParams(dimension_semantics=sem, vmem_limit_bytes=VMEM_LIMIT)


def _sigmoid(x):
    return 1.0 / (1.0 + jnp.exp(-x))


def _softplus(x):
    return jnp.maximum(x, 0.0) + jnp.log1p(jnp.exp(-jnp.abs(x)))


def _dot(a, b):
    return jnp.dot(a, b, preferred_element_type=F32)


def _dot_nt(a, b):
    return lax.dot_general(a, b, (((1,), (1,)), ((), ())), preferred_element_type=F32)


def _dot_tn(a, b):
    return lax.dot_general(a, b, (((0,), (0,)), ((), ())), preferred_element_type=F32)


def _split_bf16(x, terms):
    parts = []
    for _ in range(terms - 1):
        hi = x.astype(BF16)
        parts.append(hi)
        x = x - hi.astype(F32)
    parts.append(x.astype(BF16))
    return parts


def _dot_split(dot_fn, x, m, x_is_lhs, terms=3):
    parts = _split_bf16(x, terms)
    prods = [dot_fn(p, m) if x_is_lhs else dot_fn(m, p) for p in parts]
    out = prods[0]
    for p in prods[1:]:
        out = out + p
    return out


def _ada_kernel(c_ref, w_ref, b_ref, o_ref):
    c = c_ref[...]
    ca = (c * _sigmoid(c)).astype(BF16)
    o_ref[...] = _dot(ca, w_ref[...].astype(BF16)) + b_ref[...]


def _ada_mod(c, ada_w, ada_b):
    bsz, d = c.shape
    n = ada_w.shape[1]
    rows = 8
    c_pad = jnp.zeros((rows, d), F32).at[:bsz].set(c)
    tn = 1024
    out = pl.pallas_call(
        _ada_kernel,
        grid=(n // tn,),
        in_specs=[pl.BlockSpec((rows, d), lambda j: (0, 0)),
                  pl.BlockSpec((d, tn), lambda j: (0, j)),
                  pl.BlockSpec((1, tn), lambda j: (0, j))],
        out_specs=pl.BlockSpec((rows, tn), lambda j: (0, j)),
        out_shape=jax.ShapeDtypeStruct((rows, n), F32),
        compiler_params=_params(("parallel",)),
        name="ada_mod",
    )(c_pad, ada_w, ada_b.reshape(1, n))
    return out[:bsz]


def _small_proj_weights(mu_wag, w1, a1, g1, w_dt):
    lora = [_pad_cols(w1, LANES), _pad_cols(a1, LANES), g1]
    direct = [(1.0 - mu_wag[i])[:, None] * w for i, w in enumerate(lora)]
    shifted = [mu_wag[i][:, None] * w for i, w in enumerate(lora)]
    return jnp.concatenate(direct + [_pad_cols(w_dt, LANES)] + shifted, axis=1).astype(BF16)


def _norm1_kernel(x_ref, g_ref, sc_ref, sh_ref, ws_ref, h_ref, sm_ref, prev_ref):
    i = pl.program_id(1)

    @pl.when(i == 0)
    def _():
        prev_ref[...] = jnp.zeros_like(prev_ref)

    x = x_ref[...]
    tm = x.shape[0]
    ms = jnp.mean(x * x, axis=-1, keepdims=True)
    h = x * lax.rsqrt(ms + NORM_EPS) * g_ref[...] * (1.0 + sc_ref[...]) + sh_ref[...]
    hb = h.astype(BF16)
    h_ref[...] = hb
    p = _dot(hb, ws_ref[...])
    q = p[:, SM_WIDTH:]
    qs = pltpu.roll(q, 1, 0)
    row = lax.broadcasted_iota(jnp.int32, (8, SM_DT), 0)
    qs = jnp.concatenate([jnp.where(row == 0, prev_ref[0:1, :], qs[0:8]), qs[8:]], axis=0)
    prev_ref[0:1, :] = q[tm - 1:tm, :]
    sm_ref[:, :SM_DT] = p[:, :SM_DT] + qs
    sm_ref[:, SM_DT:] = p[:, SM_DT:SM_WIDTH]


def _norm1(x, g, sc, sh, w_small, tm=256):
    bsz, t, d = x.shape
    return pl.pallas_call(
        _norm1_kernel,
        grid=(bsz, t // tm),
        in_specs=[pl.BlockSpec((None, tm, d), lambda b, i: (b, i, 0)),
                  pl.BlockSpec((1, d), lambda b, i: (0, 0)),
                  pl.BlockSpec((None, 1, d), lambda b, i: (b, 0, 0)),
                  pl.BlockSpec((None, 1, d), lambda b, i: (b, 0, 0)),
                  pl.BlockSpec((d, SM_WIDTH + SM_DT), lambda b, i: (0, 0))],
        out_specs=[pl.BlockSpec((None, tm, d), lambda b, i: (b, i, 0)),
                   pl.BlockSpec((None, tm, SM_WIDTH), lambda b, i: (b, i, 0))],
        out_shape=[jax.ShapeDtypeStruct((bsz, t, d), BF16),
                   jax.ShapeDtypeStruct((bsz, t, SM_WIDTH), F32)],
        scratch_shapes=[pltpu.VMEM((8, SM_DT), F32)],
        compiler_params=_params(("parallel", "arbitrary")),
        name="norm1_small_proj",
    )(x, g.reshape(1, d), sc.reshape(bsz, 1, d), sh.reshape(bsz, 1, d), w_small)


def _mm_kernel(a_ref, w_ref, *rest, act, has_bias, has_side, w_transposed, shift_period):
    rest = list(rest)
    bias_ref = rest.pop(0) if has_bias else None
    mu_ref = rest.pop(0) if shift_period else None
    side_ref = rest.pop(0) if has_side else None
    o_ref = rest.pop(0)
    side_o_ref = rest.pop(0) if has_side else None
    wb_ref = rest.pop(0)
    carry_ref = rest.pop(0) if shift_period else None

    @pl.when(pl.program_id(1) == 0)
    def _():
        w = w_ref[...]
        wb_ref[...] = (w.T if w_transposed else w).astype(BF16)

    if has_side:
        side_o_ref[...] = side_ref[...].astype(BF16)

    def epilogue(y):
        if has_bias:
            y = y + bias_ref[...]
        if act == "sigmoid":
            y = _sigmoid(y)
        elif act == "relu2":
            y = jnp.maximum(y, 0.0)
            y = y * y
        return y

    if not shift_period:
        o_ref[...] = epilogue(_dot(a_ref[...], wb_ref[...])).astype(o_ref.dtype)
        return

    @pl.when(pl.program_id(1) % shift_period == 0)
    def _():
        carry_ref[...] = jnp.zeros_like(carry_ref)

    tm, tn = o_ref.shape
    sub = min(256, tm)
    row = lax.broadcasted_iota(jnp.int32, (8, tn), 0)
    before = carry_ref[0:1, :]
    for r0 in range(0, tm, sub):
        y = epilogue(_dot(a_ref[r0:r0 + sub, :], wb_ref[...]))
        ys = pltpu.roll(y, 1, 0)
        ys = jnp.concatenate([jnp.where(row == 0, before, ys[0:8]), ys[8:]], axis=0)
        before = y[sub - 1:sub, :]
        o_ref[r0:r0 + sub, :] = (y + (ys - y) * mu_ref[...]).astype(o_ref.dtype)
    carry_ref[0:1, :] = before


def _matmul(a, w, out_dtype, col0=0, n=None, bias=None, act=None, w_transposed=False, side_cast=None,
            shift_mu=None, rows_per_seq=None, tm=1024, tn=1024, name="matmul"):
    m, k = a.shape
    n = w.shape[0 if w_transposed else 1] if n is None else n
    tm, tn = min(tm, m), min(tn, n)
    assert col0 % tn == 0 and n % tn == 0 and m % tm == 0
    jb = col0 // tn
    w_spec = (pl.BlockSpec((tn, k), lambda j, i: (j + jb, 0)) if w_transposed
              else pl.BlockSpec((k, tn), lambda j, i: (0, j + jb)))
    in_specs = [pl.BlockSpec((tm, k), lambda j, i: (i, 0)), w_spec]
    args = [a, w]
    if bias is not None:
        in_specs.append(pl.BlockSpec((1, tn), lambda j, i: (0, j)))
        args.append(bias.reshape(1, n).astype(F32))
    scratch = [pltpu.VMEM((k, tn), BF16)]
    shift_period = 0
    if shift_mu is not None:
        assert rows_per_seq % tm == 0
        shift_period = rows_per_seq // tm
        in_specs.append(pl.BlockSpec((1, tn), lambda j, i: (0, j)))
        args.append(shift_mu.reshape(1, n).astype(F32))
        scratch.append(pltpu.VMEM((8, tn), F32))
    out_specs = [pl.BlockSpec((tm, tn), lambda j, i: (i, j))]
    out_shape = [jax.ShapeDtypeStruct((m, n), out_dtype)]
    if side_cast is not None:
        ni = m // tm
        rows = side_cast.shape[0] // ((n // tn) * ni)
        assert rows * (n // tn) * ni == side_cast.shape[0] and rows % 16 == 0
        side_spec = pl.BlockSpec((rows, side_cast.shape[1]), lambda j, i: (j * ni + i, 0))
        in_specs.append(side_spec)
        args.append(side_cast)
        out_specs.append(side_spec)
        out_shape.append(jax.ShapeDtypeStruct(side_cast.shape, BF16))
    outs = pl.pallas_call(
        functools.partial(_mm_kernel, act=act, has_bias=bias is not None, has_side=side_cast is not None,
                          w_transposed=w_transposed, shift_period=shift_period),
        grid=(n // tn, m // tm),
        in_specs=in_specs,
        out_specs=out_specs,
        out_shape=out_shape,
        scratch_shapes=scratch,
        compiler_params=_params(("parallel", "arbitrary")),
        name=name,
    )(*args)
    return outs if side_cast is not None else outs[0]


def _merge_kernel(oa_ref, ob_ref, wa_ref, wb_ref, ga_ref, gb_ref, side_ref, o_ref, side_o_ref):
    ya = _dot(oa_ref[...], wa_ref[...])
    yb = _dot(ob_ref[...], wb_ref[...])
    o_ref[...] = (ga_ref[...].astype(F32) * ya + gb_ref[...].astype(F32) * yb).astype(o_ref.dtype)
    side_o_ref[...] = side_ref[...].astype(BF16)


def _merge(o_a, o_b, wa, wb, gate, side_cast, tm=512, tn=1024):
    m, ka = o_a.shape
    kb = o_b.shape[1]
    n = wa.shape[1]
    off = n // tn
    ni = m // tm
    rows = side_cast.shape[0] // (off * ni)
    assert rows * off * ni == side_cast.shape[0] and rows % 16 == 0
    side_spec = pl.BlockSpec((rows, side_cast.shape[1]), lambda j, i: (j * ni + i, 0))
    return pl.pallas_call(
        _merge_kernel,
        grid=(n // tn, m // tm),
        in_specs=[pl.BlockSpec((tm, ka), lambda j, i: (i, 0)),
                  pl.BlockSpec((tm, kb), lambda j, i: (i, 0)),
                  pl.BlockSpec((ka, tn), lambda j, i: (0, j)),
                  pl.BlockSpec((kb, tn), lambda j, i: (0, j)),
                  pl.BlockSpec((tm, tn), lambda j, i: (i, j)),
                  pl.BlockSpec((tm, tn), lambda j, i: (i, j + off)),
                  side_spec],
        out_specs=[pl.BlockSpec((tm, tn), lambda j, i: (i, j)), side_spec],
        out_shape=[jax.ShapeDtypeStruct((m, n), BF16), jax.ShapeDtypeStruct(side_cast.shape, BF16)],
        compiler_params=_params(("parallel", "parallel")),
        name="branch_merge",
    )(o_a, o_b, wa, wb, gate, gate, side_cast)


def _resid_norm_kernel(a_ref, b_ref, x_ref, gt_ref, g_ref, *rest, affine, emit_x):
    rest = list(rest)
    if affine:
        sc_ref, sh_ref = rest[0], rest[1]
        rest = rest[2:]
    if emit_x:
        xo_ref, no_ref, acc_ref = rest
    else:
        no_ref, acc_ref = rest
    kk = pl.program_id(1)

    @pl.when(kk == 0)
    def _():
        acc_ref[...] = jnp.zeros_like(acc_ref)

    acc_ref[...] += _dot(a_ref[...], b_ref[...])

    @pl.when(kk == pl.num_programs(1) - 1)
    def _():
        xn = x_ref[...] + gt_ref[...] * acc_ref[...]
        if emit_x:
            xo_ref[...] = xn
        ms = jnp.mean(xn * xn, axis=-1, keepdims=True)
        y = xn * lax.rsqrt(ms + NORM_EPS) * g_ref[...]
        if affine:
            y = y * (1.0 + sc_ref[...]) + sh_ref[...]
        no_ref[...] = y.astype(no_ref.dtype)


def _resid_norm(a, b, x, gt, g, sc=None, sh=None, *, rows_per_batch, norm_dtype, emit_x, tm=512, tk=2048,
                name="resid_norm"):
    m, k = a.shape
    n = b.shape[1]
    bsz = gt.shape[0]
    tpb = rows_per_batch // tm
    affine = sc is not None
    vec = lambda v: v.reshape(bsz, 1, n)
    in_specs = [pl.BlockSpec((tm, tk), lambda i, kk: (i, kk)),
                pl.BlockSpec((tk, n), lambda i, kk: (kk, 0)),
                pl.BlockSpec((tm, n), lambda i, kk: (i, 0)),
                pl.BlockSpec((None, 1, n), lambda i, kk: (i // tpb, 0, 0)),
                pl.BlockSpec((1, n), lambda i, kk: (0, 0))]
    args = [a, b, x, vec(gt), g.reshape(1, n)]
    if affine:
        in_specs += [pl.BlockSpec((None, 1, n), lambda i, kk: (i // tpb, 0, 0))] * 2
        args += [vec(sc), vec(sh)]
    out_specs = [pl.BlockSpec((tm, n), lambda i, kk: (i, 0))]
    out_shape = [jax.ShapeDtypeStruct((m, n), norm_dtype)]
    if emit_x:
        out_specs = [pl.BlockSpec((tm, n), lambda i, kk: (i, 0))] + out_specs
        out_shape = [jax.ShapeDtypeStruct((m, n), F32)] + out_shape
    return pl.pallas_call(
        functools.partial(_resid_norm_kernel, affine=affine, emit_x=emit_x),
        grid=(m // tm, k // tk),
        in_specs=in_specs,
        out_specs=out_specs,
        out_shape=out_shape,
        scratch_shapes=[pltpu.VMEM((tm, n), F32)],
        compiler_params=_params(("parallel", "arbitrary")),
        name=name,
    )(*args)


def _seg_sum(x, ones_bd, terms):
    cols = x.shape[1]
    wid = ones_bd.shape[0]
    outs = [_dot_split(_dot, x[:, c:c + wid], ones_bd, True, terms=terms) for c in range(0, cols, wid)]
    return outs[0] if len(outs) == 1 else jnp.concatenate(outs, axis=1)


def _inv_unit_lower_all(a_list, ri, ci):
    n = a_list[0].shape[0]
    eye = (ri == ci).astype(F32)
    same16 = (ri // 16) == (ci // 16)
    same32 = (ri // 32) == (ci // 32)
    t = [eye + jnp.where(same16, a, 0.0) for a in a_list]
    p = [jnp.where(same16, a, 0.0).astype(BF16) for a in a_list]
    p = [_dot(x, x).astype(BF16) for x in p]
    for _ in range(2):
        r = [_dot(x, jnp.concatenate([tt.astype(BF16), x], axis=1)) for x, tt in zip(p, t)]
        t = [tt + rr[:, :n] for tt, rr in zip(t, r)]
        p = [rr[:, n:].astype(BF16) for rr in r]
    t = [tt + _dot(x, tt.astype(BF16)) for x, tt in zip(p, t)]
    for half, sel in ((16, same32 & (~same16)), (32, ~same32)):
        lo = [s for s in range(0, n, half) if (s // half) % 2 == 1]
        tb = [tt.astype(BF16) for tt in t]
        t_lo = [jnp.concatenate([tt[s:s + half] for s in lo], axis=0).astype(BF16) for tt in t]
        x = [_dot(tl, jnp.where(sel, a, 0.0).astype(BF16)).astype(BF16) for tl, a in zip(t_lo, a_list)]
        upd = [_dot(xx, b) for xx, b in zip(x, tb)]

        def add_rows(tt, u):
            parts = [tt[s:s + half] for s in range(0, n, half)]
            for j, s in enumerate(lo):
                parts[s // half] = parts[s // half] + u[j * half:(j + 1) * half]
            return jnp.concatenate(parts, axis=0)

        t = [add_rows(tt, u) for tt, u in zip(t, upd)]
    return t


def _rwkv_kernel(pr_ref, pk_ref, pv_ref, sm_ref, w0_ref, w2_ref, a0_ref, a2_ref,
                 g2_ref, kk_ref, ka_ref, rk_ref, lnw_ref, lnb_ref, o_ref, s_ref, y_ref):
    tb, hw = pr_ref.shape
    n_chunk = tb // RW_CHUNK
    n_pair = hw // LANES
    c_len = RW_CHUNK

    @pl.when(pl.program_id(2) == 0)
    def _():
        s_ref[...] = jnp.zeros_like(s_ref)

    r, k, v = pr_ref[...], pk_ref[...], pv_ref[...]

    lw = jnp.tanh(sm_ref[:, SM_W:SM_A]).astype(BF16)
    logw = (-math.exp(-0.5) * LOG2E) * _sigmoid(w0_ref[...] + _dot(lw, w2_ref[...]))
    a = _sigmoid(a0_ref[...] + _dot(sm_ref[:, SM_A:SM_G].astype(BF16), a2_ref[...]))
    g = _dot(_sigmoid(sm_ref[:, SM_G:SM_DT]).astype(BF16), g2_ref[...])

    li = lax.broadcasted_iota(jnp.int32, (LANES, LANES), 0)
    lj = lax.broadcasted_iota(jnp.int32, (LANES, LANES), 1)
    same_head = (li // HEAD) == (lj // HEAD)
    seg_w = min(2 * LANES, hw)
    si = lax.broadcasted_iota(jnp.int32, (seg_w, seg_w), 0)
    sj = lax.broadcasted_iota(jnp.int32, (seg_w, seg_w), 1)
    ones_bd = ((si // HEAD) == (sj // HEAD)).astype(BF16)

    kkv = k * kk_ref[...]
    kkn = kkv * lax.rsqrt(jnp.maximum(_seg_sum(kkv * kkv, ones_bd, 2), 1e-24))
    k2 = k * (1.0 + (a - 1.0) * ka_ref[...])
    aa = -kkn
    bb = kkn * a
    bonus = _seg_sum(r * k2 * rk_ref[...], ones_bd, 1) * v

    ti = lax.broadcasted_iota(jnp.int32, (tb, tb), 0)
    tj = lax.broadcasted_iota(jnp.int32, (tb, tb), 1)
    cum_mat = (((ti // c_len) == (tj // c_len)) & (tj <= ti)).astype(BF16)
    lc = _dot_split(_dot, logw, cum_mat, False, terms=2)
    e_pos = jnp.exp2(lc)
    e_neg = jnp.exp2(-lc)
    rt = r * e_pos
    at = aa * jnp.exp2(lc - logw)
    bt = bb * e_neg
    kt = k2 * e_neg

    lane = lax.broadcasted_iota(jnp.int32, (c_len, LANES), 1)
    m0 = lane < HEAD

    def bd(x):
        xb = x.astype(BF16)
        zero = jnp.zeros_like(xb)
        return jnp.concatenate([jnp.where(m0, xb, zero), jnp.where(m0, zero, xb)], axis=0)

    strict = same_head & ((li % HEAD) > (lj % HEAD))
    incl = same_head & ((li % HEAD) >= (lj % HEAD))

    inst = [(c, p) for c in range(n_chunk) for p in range(n_pair)]
    rows = lambda c: slice(c * c_len, (c + 1) * c_len)
    lanes = lambda p: slice(p * LANES, (p + 1) * LANES)
    tile = lambda x: [bd(x[rows(c), lanes(p)]) for c, p in inst]
    a_bd, r_bd, b_bd, k_bd, v_bd = tile(at), tile(rt), tile(bt), tile(kt), tile(v)

    bk_bd = [jnp.concatenate([z, w], axis=0) for z, w in zip(b_bd, k_bd)]
    gm = [_dot_nt(jnp.concatenate([x, y], axis=0), zw) for x, y, zw in zip(a_bd, r_bd, bk_bd)]
    a_ab = [jnp.where(strict, x[0:LANES, 0:LANES], 0.0) for x in gm]
    a_ak = [jnp.where(strict, x[0:LANES, LANES:], 0.0).astype(BF16) for x in gm]
    incl2 = jnp.concatenate([incl, incl], axis=1)
    a_rbk = [jnp.where(incl2, x[LANES:, :], 0.0).astype(BF16) for x in gm]
    t_inv = [x.astype(BF16) for x in _inv_unit_lower_all(a_ab, li, lj)]
    av = [_dot(x, y).astype(BF16) for x, y in zip(a_ak, v_bd)]
    tav = [_dot(t, jnp.concatenate([x, y], axis=1)) for t, x, y in zip(t_inv, a_bd, av)]
    ta = [x[:, :LANES].astype(BF16) for x in tav]
    uv = [x[:, LANES:] for x in tav]

    state = [s_ref[p] for p in range(n_pair)]
    for c in range(n_chunk):
        ids = [c * n_pair + p for p in range(n_pair)]
        s_b = [s.astype(BF16) for s in state]
        u_bd = [(_dot_nt(ta[i], s_b[p]) + uv[i]).astype(BF16) for p, i in enumerate(ids)]
        g_end = [e_pos[(c + 1) * c_len - 1:(c + 1) * c_len, lanes(p)] for p in range(n_pair)]
        uv_bd = [jnp.concatenate([u_bd[p], v_bd[i]], axis=0) for p, i in enumerate(ids)]
        state = [g_end[p] * (state[p] + _dot_tn(uv_bd[p], bk_bd[i])) for p, i in enumerate(ids)]
        for p, i in enumerate(ids):
            y_bd = _dot_nt(r_bd[i], s_b[p]) + _dot(a_rbk[i], uv_bd[p])
            y_ref[rows(c), lanes(p)] = y_bd[0:c_len] + y_bd[c_len:]
    for p in range(n_pair):
        s_ref[p] = state[p]

    y = y_ref[...]
    mu = _seg_sum(y, ones_bd, 1) * (1.0 / HEAD)
    d = y - mu
    var = _seg_sum(d * d, ones_bd, 1) * (1.0 / HEAD)
    yn = d * lax.rsqrt(var + RW_LNX_EPS) * lnw_ref[...] + lnb_ref[...]
    o_ref[...] = ((yn + bonus) * g).astype(o_ref.dtype)


def _rwkv_branch(p_rkv, small, w0, w2, a0, a2, g2, k_k, k_a, r_k, lnx_w, lnx_b, tb=256, hw=1024):
    bsz, t, w3 = p_rkv.shape
    w = w3 // 3
    nb = w // hw
    row = lambda v: v.reshape(1, -1).astype(F32)
    vec_spec = lambda off: pl.BlockSpec((1, hw), lambda b, gi, ti: (0, gi + off))
    rkv_spec = lambda off: pl.BlockSpec((None, tb, hw), lambda b, gi, ti: (b, ti, gi + off))
    mat_spec = lambda rows: pl.BlockSpec((rows, hw), lambda b, gi, ti: (0, gi))
    return pl.pallas_call(
        _rwkv_kernel,
        grid=(bsz, nb, t // tb),
        in_specs=[rkv_spec(0), rkv_spec(nb), rkv_spec(2 * nb),
                  pl.BlockSpec((None, tb, SM_WIDTH), lambda b, gi, ti: (b, ti, 0)),
                  vec_spec(0), mat_spec(LANES), vec_spec(0), mat_spec(LANES), mat_spec(RW_GATE_LORA),
                  vec_spec(0), vec_spec(0), vec_spec(0), vec_spec(0), vec_spec(0)],
        out_specs=pl.BlockSpec((None, tb, hw), lambda b, gi, ti: (b, ti, gi)),
        out_shape=jax.ShapeDtypeStruct((bsz, t, w), BF16),
        scratch_shapes=[pltpu.VMEM((hw // LANES, LANES, LANES), F32),
                        pltpu.VMEM((tb, hw), F32)],
        compiler_params=_params(("parallel", "parallel", "arbitrary")),
        name="rwkv7_branch",
    )(p_rkv, p_rkv, p_rkv, small, row(w0), w2, row(a0), a2, g2,
      row(k_k), row(k_a), row(r_k), row(lnx_w), row(lnx_b))


def _ssd_kernel(xbc_ref, z_ref, dt_ref, cw_ref, cb_ref, dtb_ref, alog_ref, dsk_ref, nw_ref, exp_ref,
                o_ref, ext_ref, act_ref, st_ref):
    ch, cw = xbc_ref.shape
    width = z_ref.shape[1]
    gw = width // MB_GROUPS
    tail = 8

    @pl.when(pl.program_id(1) == 0)
    def _():
        st_ref[...] = jnp.zeros_like(st_ref)
        ext_ref[0:tail, :] = jnp.zeros((tail, cw), F32)

    @pl.when(pl.program_id(1) != 0)
    def _():
        ext_ref[0:tail, :] = ext_ref[ch:ch + tail, :]

    ext_ref[tail:tail + ch, :] = xbc_ref[...]

    cblk = 512
    for c0 in range(0, cw, cblk):
        e = ext_ref[:, c0:c0 + cblk]
        acc = cb_ref[:, c0:c0 + cblk] + cw_ref[MB_CONV - 1:MB_CONV, c0:c0 + cblk] * e[tail:]
        for s in range(1, MB_CONV):
            acc = acc + cw_ref[MB_CONV - 1 - s:MB_CONV - s, c0:c0 + cblk] * pltpu.roll(e, s, 0)[tail:]
        act_ref[:, c0:c0 + cblk] = acc * _sigmoid(acc)

    dt = _softplus(dt_ref[...] + dtb_ref[...])
    d_a = dt * (-jnp.exp(alog_ref[...])) * LOG2E
    ti = lax.broadcasted_iota(jnp.int32, (ch, ch), 0)
    tj = lax.broadcasted_iota(jnp.int32, (ch, ch), 1)
    causal = tj <= ti
    acs = _dot_split(_dot, d_a, causal.astype(BF16), False)
    acs_t = _dot_split(_dot_tn, d_a, (ti <= tj).astype(BF16), True)
    expand = exp_ref[...]
    dt_x = _dot_split(_dot, dt, expand, True)
    acs_x = _dot_split(_dot, acs, expand, True)
    acs_last_x = acs_x[ch - 1:ch, :]
    lane = lax.broadcasted_iota(jnp.int32, (ch, LANES), 1)
    m0 = lane < HEAD

    for gi in range(MB_GROUPS):
        gs = slice(gi * gw, (gi + 1) * gw)
        c_g = act_ref[:, width + MB_GROUPS * MB_STATE + gi * MB_STATE:width + MB_GROUPS * MB_STATE + (gi + 1) * MB_STATE].astype(BF16)
        b_g = act_ref[:, width + gi * MB_STATE:width + (gi + 1) * MB_STATE].astype(BF16)
        xs_g = act_ref[:, gs]
        acs_g = acs_x[:, gs]
        x_g = xs_g * dt_x[:, gs]
        cb = _dot_nt(c_g, b_g)
        s_old = st_ref[gi]
        y_off = _dot(c_g, s_old.astype(BF16)) * jnp.exp2(acs_g)
        y_diag = []
        for q in range(gw // LANES):
            x_pair = x_g[:, q * LANES:(q + 1) * LANES]
            y_pair = None
            for hh in range(2):
                head = (gi * gw) // HEAD + 2 * q + hh
                col = jnp.broadcast_to(acs[:, head:head + 1], (ch, ch))
                rowv = acs_t[head:head + 1, :]
                l_dec = jnp.exp2(jnp.where(causal, col - rowv, -jnp.inf))
                scores = (cb * l_dec).astype(BF16)
                x_m = (jnp.where(m0, x_pair, 0.0) if hh == 0 else jnp.where(m0, 0.0, x_pair)).astype(BF16)
                part = _dot(scores, x_m)
                y_pair = part if y_pair is None else y_pair + part
            y_diag.append(y_pair)
        y = jnp.concatenate(y_diag, axis=1) + y_off + xs_g * dsk_ref[:, gs]
        last_g = acs_last_x[:, gs]
        x_dec = (x_g * jnp.exp2(last_g - acs_g)).astype(BF16)
        st_ref[gi] = s_old * jnp.exp2(last_g) + _dot_tn(b_g, x_dec)
        z = z_ref[:, gs]
        yg = y * (z * _sigmoid(z))
        ms = jnp.mean(yg * yg, axis=-1, keepdims=True)
        o_ref[:, gs] = (yg * lax.rsqrt(ms + NORM_EPS) * nw_ref[:, gs]).astype(o_ref.dtype)


def _ssd_branch(p_z, p_xbc, small, conv_w, conv_b, dt_bias, a_log, d_skip, norm_w):
    bsz, t, width = p_z.shape
    cw = p_xbc.shape[2]
    heads = width // HEAD
    ch = MB_CHUNK
    pad = lambda v: jnp.zeros((1, LANES), F32).at[0, :heads].set(v.astype(F32))
    hid = jnp.arange(LANES)[:, None]
    col = jnp.arange(width)[None, :]
    expand = (hid == col // HEAD).astype(BF16)
    d_x = jnp.repeat(d_skip.astype(F32), HEAD).reshape(1, width)
    full = lambda shape: pl.BlockSpec(shape, lambda b, c: (0, 0))
    return pl.pallas_call(
        _ssd_kernel,
        grid=(bsz, t // ch),
        in_specs=[pl.BlockSpec((None, ch, cw), lambda b, c: (b, c, 0)),
                  pl.BlockSpec((None, ch, width), lambda b, c: (b, c, 0)),
                  pl.BlockSpec((None, ch, LANES), lambda b, c: (b, c, SM_DT // LANES)),
                  full((MB_CONV, cw)), full((1, cw)), full((1, LANES)), full((1, LANES)),
                  full((1, width)), full((1, width)), full((LANES, width))],
        out_specs=pl.BlockSpec((None, ch, width), lambda b, c: (b, c, 0)),
        out_shape=jax.ShapeDtypeStruct((bsz, t, width), BF16),
        scratch_shapes=[pltpu.VMEM((ch + 8, cw), F32),
                        pltpu.VMEM((ch, cw), F32),
                        pltpu.VMEM((MB_GROUPS, MB_STATE, width // MB_GROUPS), F32)],
        compiler_params=_params(("parallel", "arbitrary")),
        name="mamba2_branch",
    )(p_xbc, p_z, small, conv_w.astype(F32), conv_b.reshape(1, cw).astype(F32), pad(dt_bias), pad(a_log),
      d_x, norm_w.reshape(1, width).astype(F32), expand)


def _pad_cols(w, n):
    return jnp.zeros((w.shape[0], n), w.dtype).at[:, :w.shape[1]].set(w)


def _pad_rows(w, n):
    return jnp.zeros((n, w.shape[1]), w.dtype).at[:w.shape[0]].set(w)


def _layer(x, c_act_mod, l, norm1_g, w_in, rw_mu_rkv, rw_mu_wag, rw_w0, rw_w1, rw_w2, rw_a0, rw_a1, rw_a2, rw_g1,
           rw_g2, rw_k_k, rw_k_a, rw_r_k, rw_lnx_w, rw_lnx_b, mb_conv_w, mb_conv_b, mb_dt_bias, mb_a_log, mb_d,
           mb_norm_w, w_branch_a, w_branch_b, w_gate, b_gate, w_out, norm2_g, mlp_up, mlp_down, out_g, out_affine):
    bsz, t, d = x.shape
    m = bsz * t
    w = rw_k_k.shape[1]
    mbw = mb_norm_w.shape[1]
    cw = mb_conv_b.shape[1]
    off_rkv, off_z, off_xbc = 3 * w, 3 * w + mbw, 3 * w + mbw + cw
    sh1, sc1, gt1, sh2, sc2, gt2 = jnp.split(c_act_mod, 6, axis=-1)

    w_small = _small_proj_weights(rw_mu_wag[l], rw_w1[l], rw_a1[l], rw_g1[l], w_in[l][:, off_xbc:])
    h, small = _norm1(x, norm1_g[l], sc1, sh1, w_small)
    h2d = h.reshape(m, d)
    w_in_t = jnp.swapaxes(w_in[l], 0, 1)
    p_rkv = _matmul(h2d, w_in_t, F32, 0, off_rkv, w_transposed=True, shift_mu=rw_mu_rkv[l], rows_per_seq=t,
                    name="proj_rkv").reshape(bsz, t, off_rkv)
    p_z, wb_bf = _matmul(h2d, w_in_t, F32, off_rkv, mbw, w_transposed=True, side_cast=w_branch_b[l], name="proj_z")
    p_z = p_z.reshape(bsz, t, mbw)
    p_xbc = _matmul(h2d, w_in_t, F32, off_z, cw, w_transposed=True, name="proj_xbc").reshape(bsz, t, cw)
    gate, wa_bf = _matmul(h2d, w_gate[l], BF16, bias=b_gate[l], act="sigmoid", side_cast=w_branch_a[l],
                          name="proj_gate")

    o_a = _rwkv_branch(p_rkv, small, rw_w0[l], _pad_rows(rw_w2[l], LANES).astype(BF16), rw_a0[l],
                       _pad_rows(rw_a2[l], LANES).astype(BF16), rw_g2[l].astype(BF16), rw_k_k[l], rw_k_a[l],
                       rw_r_k[l], rw_lnx_w[l], rw_lnx_b[l])
    o_b = _ssd_branch(p_z, p_xbc, small, mb_conv_w[l], mb_conv_b[l], mb_dt_bias[l], mb_a_log[l], mb_d[l],
                      mb_norm_w[l])
    merged, w_out_bf = _merge(o_a.reshape(m, w), o_b.reshape(m, mbw), wa_bf, wb_bf, gate, w_out[l])
    x1, h2 = _resid_norm(merged, w_out_bf, x.reshape(m, d), gt1, norm2_g[l], sc2, sh2,
                         rows_per_batch=t, norm_dtype=BF16, emit_x=True, name="out_proj_norm2")
    u, mlp_down_bf = _matmul(h2, mlp_up[l], BF16, act="relu2", side_cast=mlp_down[l], name="mlp_up")
    if out_affine is None:
        (y,) = _resid_norm(u, mlp_down_bf, x1, gt2, out_g, rows_per_batch=t, norm_dtype=F32,
                           emit_x=False, name="mlp_down_final_norm")
        return y.reshape(bsz, t, d)
    raise NotImplementedError("only a single layer followed by the final norm is implemented")


def kernel(x, c, ada_w, ada_b, norm1_g, w_in, rw_mu_rkv, rw_mu_wag, rw_w0, rw_w1, rw_w2, rw_a0, rw_a1, rw_a2, rw_g1, rw_g2, rw_k_k, rw_k_a, rw_r_k, rw_lnx_w, rw_lnx_b, mb_conv_w, mb_conv_b, mb_dt_bias, mb_a_log, mb_d, mb_norm_w, w_branch_a, w_branch_b, w_gate, b_gate, w_out, norm2_g, mlp_up, mlp_down, final_norm_g):
    depth = ada_w.shape[0]
    assert depth == 1, "the fused final-norm epilogue assumes a single layer"
    mod = _ada_mod(c, ada_w[0], ada_b[0])
    return _layer(x, mod, 0, norm1_g, w_in, rw_mu_rkv, rw_mu_wag, rw_w0, rw_w1, rw_w2, rw_a0, rw_a1, rw_a2, rw_g1,
                  rw_g2, rw_k_k, rw_k_a, rw_r_k, rw_lnx_w, rw_lnx_b, mb_conv_w, mb_conv_b, mb_dt_bias, mb_a_log,
                  mb_d, mb_norm_w, w_branch_a, w_branch_b, w_gate, b_gate, w_out, norm2_g, mlp_up, mlp_down,
                  final_norm_g, None)
```

```python
import functools
import math

import jax
import jax.numpy as jnp
from jax import lax
from jax.experimental import pallas as pl
from jax.experimental.pallas import tpu as pltpu

F32 = jnp.float32
BF16 = jnp.bfloat16

LANES = 128
HEAD = 64
RW_CHUNK = 64
RW_DECAY_LORA = 96
RW_AAA_LORA = 96
RW_GATE_LORA = 256
RW_LNX_EPS = 64e-5
MB_GROUPS = 8
MB_STATE = 128
MB_CONV = 4
MB_CHUNK = 128
NORM_EPS = 1e-5
LOG2E = 1.4426950408889634
VMEM_LIMIT = 56 * 1024 * 1024

SM_W, SM_A, SM_G, SM_DT, SM_WIDTH = 0, 128, 256, 512, 640


def _params(sem):
    return pltpu.CompilerParams(dimension_semantics=sem, vmem_limit_bytes=VMEM_LIMIT)


def _sigmoid(x):
    return 1.0 / (1.0 + jnp.exp(-x))


def _softplus(x):
    return jnp.maximum(x, 0.0) + jnp.log1p(jnp.exp(-jnp.abs(x)))


def _dot(a, b):
    return jnp.dot(a, b, preferred_element_type=F32)


def _dot_nt(a, b):
    return lax.dot_general(a, b, (((1,), (1,)), ((), ())), preferred_element_type=F32)


def _dot_tn(a, b):
    return lax.dot_general(a, b, (((0,), (0,)), ((), ())), preferred_element_type=F32)


def _split_bf16(x, terms):
    parts = []
    for _ in range(terms - 1):
        hi = x.astype(BF16)
        parts.append(hi)
        x = x - hi.astype(F32)
    parts.append(x.astype(BF16))
    return parts


def _dot_split(dot_fn, x, m, x_is_lhs, terms=3):
    parts = _split_bf16(x, terms)
    prods = [dot_fn(p, m) if x_is_lhs else dot_fn(m, p) for p in parts]
    out = prods[0]
    for p in prods[1:]:
        out = out + p
    return out


def _ada_kernel(c_ref, w_ref, b_ref, o_ref):
    c = c_ref[...]
    ca = (c * _sigmoid(c)).astype(BF16)
    o_ref[...] = _dot(ca, w_ref[...].astype(BF16)) + b_ref[...]


def _ada_mod(c, ada_w, ada_b):
    bsz, d = c.shape
    n = ada_w.shape[1]
    rows = 8
    c_pad = jnp.zeros((rows, d), F32).at[:bsz].set(c)
    tn = 1024
    out = pl.pallas_call(
        _ada_kernel,
        grid=(n // tn,),
        in_specs=[pl.BlockSpec((rows, d), lambda j: (0, 0)),
                  pl.BlockSpec((d, tn), lambda j: (0, j)),
                  pl.BlockSpec((1, tn), lambda j: (0, j))],
        out_specs=pl.BlockSpec((rows, tn), lambda j: (0, j)),
        out_shape=jax.ShapeDtypeStruct((rows, n), F32),
        compiler_params=_params(("parallel",)),
        name="ada_mod",
    )(c_pad, ada_w, ada_b.reshape(1, n))
    return out[:bsz]


def _small_proj_weights(mu_wag, w1, a1, g1, w_dt):
    lora = [_pad_cols(w1, LANES), _pad_cols(a1, LANES), g1]
    direct = [(1.0 - mu_wag[i])[:, None] * w for i, w in enumerate(lora)]
    shifted = [mu_wag[i][:, None] * w for i, w in enumerate(lora)]
    return jnp.concatenate(direct + [_pad_cols(w_dt, LANES)] + shifted, axis=1).astype(BF16)


def _norm1_kernel(x_ref, g_ref, sc_ref, sh_ref, ws_ref, h_ref, sm_ref, prev_ref):
    i = pl.program_id(1)

    @pl.when(i == 0)
    def _():
        prev_ref[...] = jnp.zeros_like(prev_ref)

    x = x_ref[...]
    tm = x.shape[0]
    ms = jnp.mean(x * x, axis=-1, keepdims=True)
    h = x * lax.rsqrt(ms + NORM_EPS) * g_ref[...] * (1.0 + sc_ref[...]) + sh_ref[...]
    hb = h.astype(BF16)
    h_ref[...] = hb
    p = _dot(hb, ws_ref[...])
    q = p[:, SM_WIDTH:]
    qs = pltpu.roll(q, 1, 0)
    row = lax.broadcasted_iota(jnp.int32, (8, SM_DT), 0)
    qs = jnp.concatenate([jnp.where(row == 0, prev_ref[0:1, :], qs[0:8]), qs[8:]], axis=0)
    prev_ref[0:1, :] = q[tm - 1:tm, :]
    sm_ref[:, :SM_DT] = p[:, :SM_DT] + qs
    sm_ref[:, SM_DT:] = p[:, SM_DT:SM_WIDTH]


def _norm1(x, g, sc, sh, w_small, tm=256):
    bsz, t, d = x.shape
    return pl.pallas_call(
        _norm1_kernel,
        grid=(bsz, t // tm),
        in_specs=[pl.BlockSpec((None, tm, d), lambda b, i: (b, i, 0)),
                  pl.BlockSpec((1, d), lambda b, i: (0, 0)),
                  pl.BlockSpec((None, 1, d), lambda b, i: (b, 0, 0)),
                  pl.BlockSpec((None, 1, d), lambda b, i: (b, 0, 0)),
                  pl.BlockSpec((d, SM_WIDTH + SM_DT), lambda b, i: (0, 0))],
        out_specs=[pl.BlockSpec((None, tm, d), lambda b, i: (b, i, 0)),
                   pl.BlockSpec((None, tm, SM_WIDTH), lambda b, i: (b, i, 0))],
        out_shape=[jax.ShapeDtypeStruct((bsz, t, d), BF16),
                   jax.ShapeDtypeStruct((bsz, t, SM_WIDTH), F32)],
        scratch_shapes=[pltpu.VMEM((8, SM_DT), F32)],
        compiler_params=_params(("parallel", "arbitrary")),
        name="norm1_small_proj",
    )(x, g.reshape(1, d), sc.reshape(bsz, 1, d), sh.reshape(bsz, 1, d), w_small)


def _mm_kernel(a_ref, w_ref, *rest, act, has_bias, has_side, w_transposed, shift_period):
    rest = list(rest)
    bias_ref = rest.pop(0) if has_bias else None
    mu_ref = rest.pop(0) if shift_period else None
    side_ref = rest.pop(0) if has_side else None
    o_ref = rest.pop(0)
    side_o_ref = rest.pop(0) if has_side else None
    wb_ref = rest.pop(0)
    carry_ref = rest.pop(0) if shift_period else None

    @pl.when(pl.program_id(1) == 0)
    def _():
        w = w_ref[...]
        wb_ref[...] = (w.T if w_transposed else w).astype(BF16)

    if has_side:
        side_o_ref[...] = side_ref[...].astype(BF16)

    def epilogue(y):
        if has_bias:
            y = y + bias_ref[...]
        if act == "sigmoid":
            y = _sigmoid(y)
        elif act == "relu2":
            y = jnp.maximum(y, 0.0)
            y = y * y
        return y

    if not shift_period:
        tm = o_ref.shape[0]
        sub = min(256, tm)
        for r0 in range(0, tm, sub):
            o_ref[r0:r0 + sub, :] = epilogue(_dot(a_ref[r0:r0 + sub, :], wb_ref[...])).astype(o_ref.dtype)
        return

    @pl.when(pl.program_id(1) % shift_period == 0)
    def _():
        carry_ref[...] = jnp.zeros_like(carry_ref)

    tm, tn = o_ref.shape
    sub = min(256, tm)
    row = lax.broadcasted_iota(jnp.int32, (8, tn), 0)
    before = carry_ref[0:1, :]
    for r0 in range(0, tm, sub):
        y = epilogue(_dot(a_ref[r0:r0 + sub, :], wb_ref[...]))
        ys = pltpu.roll(y, 1, 0)
        ys = jnp.concatenate([jnp.where(row == 0, before, ys[0:8]), ys[8:]], axis=0)
        before = y[sub - 1:sub, :]
        o_ref[r0:r0 + sub, :] = (y + (ys - y) * mu_ref[...]).astype(o_ref.dtype)
    carry_ref[0:1, :] = before


def _matmul(a, w, out_dtype, col0=0, n=None, bias=None, act=None, w_transposed=False, side_cast=None,
            shift_mu=None, rows_per_seq=None, tm=1024, tn=1024, name="matmul"):
    m, k = a.shape
    n = w.shape[0 if w_transposed else 1] if n is None else n
    tm, tn = min(tm, m), min(tn, n)
    assert col0 % tn == 0 and n % tn == 0 and m % tm == 0
    jb = col0 // tn
    w_spec = (pl.BlockSpec((tn, k), lambda j, i: (j + jb, 0)) if w_transposed
              else pl.BlockSpec((k, tn), lambda j, i: (0, j + jb)))
    in_specs = [pl.BlockSpec((tm, k), lambda j, i: (i, 0)), w_spec]
    args = [a, w]
    if bias is not None:
        in_specs.append(pl.BlockSpec((1, tn), lambda j, i: (0, j)))
        args.append(bias.reshape(1, n).astype(F32))
    scratch = [pltpu.VMEM((k, tn), BF16)]
    shift_period = 0
    if shift_mu is not None:
        assert rows_per_seq % tm == 0
        shift_period = rows_per_seq // tm
        in_specs.append(pl.BlockSpec((1, tn), lambda j, i: (0, j)))
        args.append(shift_mu.reshape(1, n).astype(F32))
        scratch.append(pltpu.VMEM((8, tn), F32))
    out_specs = [pl.BlockSpec((tm, tn), lambda j, i: (i, j))]
    out_shape = [jax.ShapeDtypeStruct((m, n), out_dtype)]
    if side_cast is not None:
        ni = m // tm
        rows = side_cast.shape[0] // ((n // tn) * ni)
        assert rows * (n // tn) * ni == side_cast.shape[0] and rows % 16 == 0
        side_spec = pl.BlockSpec((rows, side_cast.shape[1]), lambda j, i: (j * ni + i, 0))
        in_specs.append(side_spec)
        args.append(side_cast)
        out_specs.append(side_spec)
        out_shape.append(jax.ShapeDtypeStruct(side_cast.shape, BF16))
    outs = pl.pallas_call(
        functools.partial(_mm_kernel, act=act, has_bias=bias is not None, has_side=side_cast is not None,
                          w_transposed=w_transposed, shift_period=shift_period),
        grid=(n // tn, m // tm),
        in_specs=in_specs,
        out_specs=out_specs,
        out_shape=out_shape,
        scratch_shapes=scratch,
        compiler_params=_params(("parallel", "arbitrary")),
        name=name,
    )(*args)
    return outs if side_cast is not None else outs[0]


def _merge_kernel(oa_ref, ob_ref, wa_ref, wb_ref, ga_ref, gb_ref, side_ref, o_ref, side_o_ref):
    ya = _dot(oa_ref[...], wa_ref[...])
    yb = _dot(ob_ref[...], wb_ref[...])
    o_ref[...] = (ga_ref[...].astype(F32) * ya + gb_ref[...].astype(F32) * yb).astype(o_ref.dtype)
    side_o_ref[...] = side_ref[...].astype(BF16)


def _merge(o_a, o_b, wa, wb, gate, side_cast, tm=512, tn=1024):
    m, ka = o_a.shape
    kb = o_b.shape[1]
    n = wa.shape[1]
    off = n // tn
    ni = m // tm
    rows = side_cast.shape[0] // (off * ni)
    assert rows * off * ni == side_cast.shape[0] and rows % 16 == 0
    side_spec = pl.BlockSpec((rows, side_cast.shape[1]), lambda j, i: (j * ni + i, 0))
    return pl.pallas_call(
        _merge_kernel,
        grid=(n // tn, m // tm),
        in_specs=[pl.BlockSpec((tm, ka), lambda j, i: (i, 0)),
                  pl.BlockSpec((tm, kb), lambda j, i: (i, 0)),
                  pl.BlockSpec((ka, tn), lambda j, i: (0, j)),
                  pl.BlockSpec((kb, tn), lambda j, i: (0, j)),
                  pl.BlockSpec((tm, tn), lambda j, i: (i, j)),
                  pl.BlockSpec((tm, tn), lambda j, i: (i, j + off)),
                  side_spec],
        out_specs=[pl.BlockSpec((tm, tn), lambda j, i: (i, j)), side_spec],
        out_shape=[jax.ShapeDtypeStruct((m, n), BF16), jax.ShapeDtypeStruct(side_cast.shape, BF16)],
        compiler_params=_params(("parallel", "parallel")),
        name="branch_merge",
    )(o_a, o_b, wa, wb, gate, gate, side_cast)


def _resid_norm_kernel(a_ref, b_ref, x_ref, gt_ref, g_ref, *rest, affine, emit_x):
    rest = list(rest)
    if affine:
        sc_ref, sh_ref = rest[0], rest[1]
        rest = rest[2:]
    if emit_x:
        xo_ref, no_ref, acc_ref = rest
    else:
        no_ref, acc_ref = rest
    kk = pl.program_id(1)

    @pl.when(kk == 0)
    def _():
        acc_ref[...] = jnp.zeros_like(acc_ref)

    acc_ref[...] += _dot(a_ref[...], b_ref[...])

    @pl.when(kk == pl.num_programs(1) - 1)
    def _():
        xn = x_ref[...] + gt_ref[...] * acc_ref[...]
        if emit_x:
            xo_ref[...] = xn
        ms = jnp.mean(xn * xn, axis=-1, keepdims=True)
        y = xn * lax.rsqrt(ms + NORM_EPS) * g_ref[...]
        if affine:
            y = y * (1.0 + sc_ref[...]) + sh_ref[...]
        no_ref[...] = y.astype(no_ref.dtype)


def _resid_norm(a, b, x, gt, g, sc=None, sh=None, *, rows_per_batch, norm_dtype, emit_x, tm=512, tk=2048,
                name="resid_norm"):
    m, k = a.shape
    n = b.shape[1]
    bsz = gt.shape[0]
    tpb = rows_per_batch // tm
    affine = sc is not None
    vec = lambda v: v.reshape(bsz, 1, n)
    in_specs = [pl.BlockSpec((tm, tk), lambda i, kk: (i, kk)),
                pl.BlockSpec((tk, n), lambda i, kk: (kk, 0)),
                pl.BlockSpec((tm, n), lambda i, kk: (i, 0)),
                pl.BlockSpec((None, 1, n), lambda i, kk: (i // tpb, 0, 0)),
                pl.BlockSpec((1, n), lambda i, kk: (0, 0))]
    args = [a, b, x, vec(gt), g.reshape(1, n)]
    if affine:
        in_specs += [pl.BlockSpec((None, 1, n), lambda i, kk: (i // tpb, 0, 0))] * 2
        args += [vec(sc), vec(sh)]
    out_specs = [pl.BlockSpec((tm, n), lambda i, kk: (i, 0))]
    out_shape = [jax.ShapeDtypeStruct((m, n), norm_dtype)]
    if emit_x:
        out_specs = [pl.BlockSpec((tm, n), lambda i, kk: (i, 0))] + out_specs
        out_shape = [jax.ShapeDtypeStruct((m, n), F32)] + out_shape
    return pl.pallas_call(
        functools.partial(_resid_norm_kernel, affine=affine, emit_x=emit_x),
        grid=(m // tm, k // tk),
        in_specs=in_specs,
        out_specs=out_specs,
        out_shape=out_shape,
        scratch_shapes=[pltpu.VMEM((tm, n), F32)],
        compiler_params=_params(("parallel", "arbitrary")),
        name=name,
    )(*args)


def _seg_sum(x, ones_bd, terms):
    cols = x.shape[1]
    wid = ones_bd.shape[0]
    outs = [_dot_split(_dot, x[:, c:c + wid], ones_bd, True, terms=terms) for c in range(0, cols, wid)]
    return outs[0] if len(outs) == 1 else jnp.concatenate(outs, axis=1)


def _inv_unit_lower_all(a_list, ri, ci):
    n = a_list[0].shape[0]
    eye = (ri == ci).astype(F32)
    same16 = (ri // 16) == (ci // 16)
    same32 = (ri // 32) == (ci // 32)
    t = [eye + jnp.where(same16, a, 0.0) for a in a_list]
    p = [jnp.where(same16, a, 0.0).astype(BF16) for a in a_list]
    p = [_dot(x, x).astype(BF16) for x in p]
    for _ in range(2):
        r = [_dot(x, jnp.concatenate([tt.astype(BF16), x], axis=1)) for x, tt in zip(p, t)]
        t = [tt + rr[:, :n] for tt, rr in zip(t, r)]
        p = [rr[:, n:].astype(BF16) for rr in r]
    t = [tt + _dot(x, tt.astype(BF16)) for x, tt in zip(p, t)]
    for half, sel in ((16, same32 & (~same16)), (32, ~same32)):
        lo = [s for s in range(0, n, half) if (s // half) % 2 == 1]
        tb = [tt.astype(BF16) for tt in t]
        t_lo = [jnp.concatenate([tt[s:s + half] for s in lo], axis=0).astype(BF16) for tt in t]
        x = [_dot(tl, jnp.where(sel, a, 0.0).astype(BF16)).astype(BF16) for tl, a in zip(t_lo, a_list)]
        upd = [_dot(xx, b) for xx, b in zip(x, tb)]

        def add_rows(tt, u):
            parts = [tt[s:s + half] for s in range(0, n, half)]
            for j, s in enumerate(lo):
                parts[s // half] = parts[s // half] + u[j * half:(j + 1) * half]
            return jnp.concatenate(parts, axis=0)

        t = [add_rows(tt, u) for tt, u in zip(t, upd)]
    return t


def _rwkv_kernel(pr_ref, pk_ref, pv_ref, sm_ref, w0_ref, w2_ref, a0_ref, a2_ref,
                 g2_ref, kk_ref, ka_ref, rk_ref, lnw_ref, lnb_ref, o_ref, s_ref, y_ref):
    tb, hw = pr_ref.shape
    n_chunk = tb // RW_CHUNK
    n_pair = hw // LANES
    c_len = RW_CHUNK

    @pl.when(pl.program_id(2) == 0)
    def _():
        s_ref[...] = jnp.zeros_like(s_ref)

    r, k, v = pr_ref[...], pk_ref[...], pv_ref[...]

    lw = jnp.tanh(sm_ref[:, SM_W:SM_A]).astype(BF16)
    logw = (-math.exp(-0.5) * LOG2E) * _sigmoid(w0_ref[...] + _dot(lw, w2_ref[...]))
    a = _sigmoid(a0_ref[...] + _dot(sm_ref[:, SM_A:SM_G].astype(BF16), a2_ref[...]))
    g = _dot(_sigmoid(sm_ref[:, SM_G:SM_DT]).astype(BF16), g2_ref[...])

    li = lax.broadcasted_iota(jnp.int32, (LANES, LANES), 0)
    lj = lax.broadcasted_iota(jnp.int32, (LANES, LANES), 1)
    same_head = (li // HEAD) == (lj // HEAD)
    seg_w = min(2 * LANES, hw)
    si = lax.broadcasted_iota(jnp.int32, (seg_w, seg_w), 0)
    sj = lax.broadcasted_iota(jnp.int32, (seg_w, seg_w), 1)
    ones_bd = ((si // HEAD) == (sj // HEAD)).astype(BF16)

    kkv = k * kk_ref[...]
    kkn = kkv * lax.rsqrt(jnp.maximum(_seg_sum(kkv * kkv, ones_bd, 2), 1e-24))
    k2 = k * (1.0 + (a - 1.0) * ka_ref[...])
    aa = -kkn
    bb = kkn * a
    bonus = _seg_sum(r * k2 * rk_ref[...], ones_bd, 1) * v

    ti = lax.broadcasted_iota(jnp.int32, (tb, tb), 0)
    tj = lax.broadcasted_iota(jnp.int32, (tb, tb), 1)
    cum_mat = (((ti // c_len) == (tj // c_len)) & (tj <= ti)).astype(BF16)
    lc = _dot_split(_dot, logw, cum_mat, False, terms=2)
    e_pos = jnp.exp2(lc)
    e_neg = jnp.exp2(-lc)
    rt = r * e_pos
    at = aa * jnp.exp2(lc - logw)
    bt = bb * e_neg
    kt = k2 * e_neg

    lane = lax.broadcasted_iota(jnp.int32, (c_len, LANES), 1)
    m0 = lane < HEAD

    def bd(x):
        xb = x.astype(BF16)
        zero = jnp.zeros_like(xb)
        return jnp.concatenate([jnp.where(m0, xb, zero), jnp.where(m0, zero, xb)], axis=0)

    strict = same_head & ((li % HEAD) > (lj % HEAD))
    incl = same_head & ((li % HEAD) >= (lj % HEAD))

    inst = [(c, p) for c in range(n_chunk) for p in range(n_pair)]
    rows = lambda c: slice(c * c_len, (c + 1) * c_len)
    lanes = lambda p: slice(p * LANES, (p + 1) * LANES)
    tile = lambda x: [bd(x[rows(c), lanes(p)]) for c, p in inst]
    a_bd, r_bd, b_bd, k_bd, v_bd = tile(at), tile(rt), tile(bt), tile(kt), tile(v)

    bk_bd = [jnp.concatenate([z, w], axis=0) for z, w in zip(b_bd, k_bd)]
    gm = [_dot_nt(jnp.concatenate([x, y], axis=0), zw) for x, y, zw in zip(a_bd, r_bd, bk_bd)]
    a_ab = [jnp.where(strict, x[0:LANES, 0:LANES], 0.0) for x in gm]
    a_ak = [jnp.where(strict, x[0:LANES, LANES:], 0.0).astype(BF16) for x in gm]
    incl2 = jnp.concatenate([incl, incl], axis=1)
    a_rbk = [jnp.where(incl2, x[LANES:, :], 0.0).astype(BF16) for x in gm]
    t_inv = [x.astype(BF16) for x in _inv_unit_lower_all(a_ab, li, lj)]
    av = [_dot(x, y).astype(BF16) for x, y in zip(a_ak, v_bd)]
    tav = [_dot(t, jnp.concatenate([x, y], axis=1)) for t, x, y in zip(t_inv, a_bd, av)]
    ta = [x[:, :LANES].astype(BF16) for x in tav]
    uv = [x[:, LANES:] for x in tav]

    state = [s_ref[p] for p in range(n_pair)]
    for c in range(n_chunk):
        ids = [c * n_pair + p for p in range(n_pair)]
        s_b = [s.astype(BF16) for s in state]
        u_bd = [(_dot_nt(ta[i], s_b[p]) + uv[i]).astype(BF16) for p, i in enumerate(ids)]
        g_end = [e_pos[(c + 1) * c_len - 1:(c + 1) * c_len, lanes(p)] for p in range(n_pair)]
        uv_bd = [jnp.concatenate([u_bd[p], v_bd[i]], axis=0) for p, i in enumerate(ids)]
        state = [g_end[p] * (state[p] + _dot_tn(uv_bd[p], bk_bd[i])) for p, i in enumerate(ids)]
        for p, i in enumerate(ids):
            y_bd = _dot_nt(r_bd[i], s_b[p]) + _dot(a_rbk[i], uv_bd[p])
            y_ref[rows(c), lanes(p)] = y_bd[0:c_len] + y_bd[c_len:]
    for p in range(n_pair):
        s_ref[p] = state[p]

    y = y_ref[...]
    mu = _seg_sum(y, ones_bd, 1) * (1.0 / HEAD)
    d = y - mu
    var = _seg_sum(d * d, ones_bd, 1) * (1.0 / HEAD)
    yn = d * lax.rsqrt(var + RW_LNX_EPS) * lnw_ref[...] + lnb_ref[...]
    o_ref[...] = ((yn + bonus) * g).astype(o_ref.dtype)


def _rwkv_branch(p_rkv, small, w0, w2, a0, a2, g2, k_k, k_a, r_k, lnx_w, lnx_b, tb=256, hw=1024):
    bsz, t, w3 = p_rkv.shape
    w = w3 // 3
    nb = w // hw
    row = lambda v: v.reshape(1, -1).astype(F32)
    vec_spec = lambda off: pl.BlockSpec((1, hw), lambda b, gi, ti: (0, gi + off))
    rkv_spec = lambda off: pl.BlockSpec((None, tb, hw), lambda b, gi, ti: (b, ti, gi + off))
    mat_spec = lambda rows: pl.BlockSpec((rows, hw), lambda b, gi, ti: (0, gi))
    return pl.pallas_call(
        _rwkv_kernel,
        grid=(bsz, nb, t // tb),
        in_specs=[rkv_spec(0), rkv_spec(nb), rkv_spec(2 * nb),
                  pl.BlockSpec((None, tb, SM_WIDTH), lambda b, gi, ti: (b, ti, 0)),
                  vec_spec(0), mat_spec(LANES), vec_spec(0), mat_spec(LANES), mat_spec(RW_GATE_LORA),
                  vec_spec(0), vec_spec(0), vec_spec(0), vec_spec(0), vec_spec(0)],
        out_specs=pl.BlockSpec((None, tb, hw), lambda b, gi, ti: (b, ti, gi)),
        out_shape=jax.ShapeDtypeStruct((bsz, t, w), BF16),
        scratch_shapes=[pltpu.VMEM((hw // LANES, LANES, LANES), F32),
                        pltpu.VMEM((tb, hw), F32)],
        compiler_params=_params(("parallel", "parallel", "arbitrary")),
        name="rwkv7_branch",
    )(p_rkv, p_rkv, p_rkv, small, row(w0), w2, row(a0), a2, g2,
      row(k_k), row(k_a), row(r_k), row(lnx_w), row(lnx_b))


def _ssd_kernel(xbc_ref, z_ref, dt_ref, cw_ref, cb_ref, dtb_ref, alog_ref, dsk_ref, nw_ref, exp_ref,
                o_ref, ext_ref, act_ref, st_ref):
    ch, cw = xbc_ref.shape
    width = z_ref.shape[1]
    gw = width // MB_GROUPS
    tail = 8

    @pl.when(pl.program_id(1) == 0)
    def _():
        st_ref[...] = jnp.zeros_like(st_ref)
        ext_ref[0:tail, :] = jnp.zeros((tail, cw), F32)

    @pl.when(pl.program_id(1) != 0)
    def _():
        ext_ref[0:tail, :] = ext_ref[ch:ch + tail, :]

    ext_ref[tail:tail + ch, :] = xbc_ref[...]

    cblk = 512
    for c0 in range(0, cw, cblk):
        e = ext_ref[:, c0:c0 + cblk]
        acc = cb_ref[:, c0:c0 + cblk] + cw_ref[MB_CONV - 1:MB_CONV, c0:c0 + cblk] * e[tail:]
        for s in range(1, MB_CONV):
            acc = acc + cw_ref[MB_CONV - 1 - s:MB_CONV - s, c0:c0 + cblk] * pltpu.roll(e, s, 0)[tail:]
        act_ref[:, c0:c0 + cblk] = acc * _sigmoid(acc)

    dt = _softplus(dt_ref[...] + dtb_ref[...])
    d_a = dt * (-jnp.exp(alog_ref[...])) * LOG2E
    ti = lax.broadcasted_iota(jnp.int32, (ch, ch), 0)
    tj = lax.broadcasted_iota(jnp.int32, (ch, ch), 1)
    causal = tj <= ti
    acs = _dot_split(_dot, d_a, causal.astype(BF16), False)
    acs_t = _dot_split(_dot_tn, d_a, (ti <= tj).astype(BF16), True)
    expand = exp_ref[...]
    dt_x = _dot_split(_dot, dt, expand, True)
    acs_x = _dot_split(_dot, acs, expand, True)
    acs_last_x = acs_x[ch - 1:ch, :]
    lane = lax.broadcasted_iota(jnp.int32, (ch, LANES), 1)
    m0 = lane < HEAD

    for gi in range(MB_GROUPS):
        gs = slice(gi * gw, (gi + 1) * gw)
        c_g = act_ref[:, width + MB_GROUPS * MB_STATE + gi * MB_STATE:width + MB_GROUPS * MB_STATE + (gi + 1) * MB_STATE].astype(BF16)
        b_g = act_ref[:, width + gi * MB_STATE:width + (gi + 1) * MB_STATE].astype(BF16)
        xs_g = act_ref[:, gs]
        acs_g = acs_x[:, gs]
        x_g = (xs_g * dt_x[:, gs]).astype(BF16)
        cb = _dot_nt(c_g, b_g).astype(BF16)
        s_old = st_ref[gi]
        y_off = _dot(c_g, s_old.astype(BF16)) * jnp.exp2(acs_g)
        y_diag = []
        for q in range(gw // LANES):
            x_pair = x_g[:, q * LANES:(q + 1) * LANES]
            x_zero = jnp.zeros_like(x_pair)
            y_pair = None
            for hh in range(2):
                head = (gi * gw) // HEAD + 2 * q + hh
                col = jnp.broadcast_to(acs[:, head:head + 1], (ch, ch))
                rowv = acs_t[head:head + 1, :]
                l_dec = jnp.exp2(jnp.where(causal, col - rowv, -jnp.inf))
                scores = cb * l_dec.astype(BF16)
                x_m = jnp.where(m0, x_pair, x_zero) if hh == 0 else jnp.where(m0, x_zero, x_pair)
                part = _dot(scores, x_m)
                y_pair = part if y_pair is None else y_pair + part
            y_diag.append(y_pair)
        y = jnp.concatenate(y_diag, axis=1) + y_off + xs_g * dsk_ref[:, gs]
        last_g = acs_last_x[:, gs]
        x_dec = x_g * jnp.exp2(last_g - acs_g).astype(BF16)
        st_ref[gi] = s_old * jnp.exp2(last_g) + _dot_tn(b_g, x_dec)
        z = z_ref[:, gs]
        yg = y * (z * _sigmoid(z))
        ms = jnp.mean(yg * yg, axis=-1, keepdims=True)
        o_ref[:, gs] = (yg * lax.rsqrt(ms + NORM_EPS) * nw_ref[:, gs]).astype(o_ref.dtype)


def _ssd_branch(p_z, p_xbc, small, conv_w, conv_b, dt_bias, a_log, d_skip, norm_w):
    bsz, t, width = p_z.shape
    cw = p_xbc.shape[2]
    heads = width // HEAD
    ch = MB_CHUNK
    pad = lambda v: jnp.zeros((1, LANES), F32).at[0, :heads].set(v.astype(F32))
    hid = jnp.arange(LANES)[:, None]
    col = jnp.arange(width)[None, :]
    expand = (hid == col // HEAD).astype(BF16)
    d_x = jnp.repeat(d_skip.astype(F32), HEAD).reshape(1, width)
    full = lambda shape: pl.BlockSpec(shape, lambda b, c: (0, 0))
    return pl.pallas_call(
        _ssd_kernel,
        grid=(bsz, t // ch),
        in_specs=[pl.BlockSpec((None, ch, cw), lambda b, c: (b, c, 0)),
                  pl.BlockSpec((None, ch, width), lambda b, c: (b, c, 0)),
                  pl.BlockSpec((None, ch, LANES), lambda b, c: (b, c, SM_DT // LANES)),
                  full((MB_CONV, cw)), full((1, cw)), full((1, LANES)), full((1, LANES)),
                  full((1, width)), full((1, width)), full((LANES, width))],
        out_specs=pl.BlockSpec((None, ch, width), lambda b, c: (b, c, 0)),
        out_shape=jax.ShapeDtypeStruct((bsz, t, width), BF16),
        scratch_shapes=[pltpu.VMEM((ch + 8, cw), F32),
                        pltpu.VMEM((ch, cw), F32),
                        pltpu.VMEM((MB_GROUPS, MB_STATE, width // MB_GROUPS), F32)],
        compiler_params=_params(("parallel", "arbitrary")),
        name="mamba2_branch",
    )(p_xbc, p_z, small, conv_w.astype(F32), conv_b.reshape(1, cw).astype(F32), pad(dt_bias), pad(a_log),
      d_x, norm_w.reshape(1, width).astype(F32), expand)


def _pad_cols(w, n):
    return jnp.zeros((w.shape[0], n), w.dtype).at[:, :w.shape[1]].set(w)


def _pad_rows(w, n):
    return jnp.zeros((n, w.shape[1]), w.dtype).at[:w.shape[0]].set(w)


def _layer(x, c_act_mod, l, norm1_g, w_in, rw_mu_rkv, rw_mu_wag, rw_w0, rw_w1, rw_w2, rw_a0, rw_a1, rw_a2, rw_g1,
           rw_g2, rw_k_k, rw_k_a, rw_r_k, rw_lnx_w, rw_lnx_b, mb_conv_w, mb_conv_b, mb_dt_bias, mb_a_log, mb_d,
           mb_norm_w, w_branch_a, w_branch_b, w_gate, b_gate, w_out, norm2_g, mlp_up, mlp_down, out_g, out_affine):
    bsz, t, d = x.shape
    m = bsz * t
    w = rw_k_k.shape[1]
    mbw = mb_norm_w.shape[1]
    cw = mb_conv_b.shape[1]
    off_rkv, off_z, off_xbc = 3 * w, 3 * w + mbw, 3 * w + mbw + cw
    sh1, sc1, gt1, sh2, sc2, gt2 = jnp.split(c_act_mod, 6, axis=-1)

    w_small = _small_proj_weights(rw_mu_wag[l], rw_w1[l], rw_a1[l], rw_g1[l], w_in[l][:, off_xbc:])
    h, small = _norm1(x, norm1_g[l], sc1, sh1, w_small)
    h2d = h.reshape(m, d)
    w_in_t = jnp.swapaxes(w_in[l], 0, 1)
    p_rkv = _matmul(h2d, w_in_t, F32, 0, off_rkv, w_transposed=True, shift_mu=rw_mu_rkv[l], rows_per_seq=t,
                    name="proj_rkv").reshape(bsz, t, off_rkv)
    p_z, wb_bf = _matmul(h2d, w_in_t, F32, off_rkv, mbw, w_transposed=True, side_cast=w_branch_b[l], name="proj_z")
    p_z = p_z.reshape(bsz, t, mbw)
    p_xbc = _matmul(h2d, w_in_t, F32, off_z, cw, w_transposed=True, name="proj_xbc").reshape(bsz, t, cw)
    gate, wa_bf = _matmul(h2d, w_gate[l], BF16, bias=b_gate[l], act="sigmoid", side_cast=w_branch_a[l],
                          name="proj_gate")

    o_a = _rwkv_branch(p_rkv, small, rw_w0[l], _pad_rows(rw_w2[l], LANES).astype(BF16), rw_a0[l],
                       _pad_rows(rw_a2[l], LANES).astype(BF16), rw_g2[l].astype(BF16), rw_k_k[l], rw_k_a[l],
                       rw_r_k[l], rw_lnx_w[l], rw_lnx_b[l])
    o_b = _ssd_branch(p_z, p_xbc, small, mb_conv_w[l], mb_conv_b[l], mb_dt_bias[l], mb_a_log[l], mb_d[l],
                      mb_norm_w[l])
    merged, w_out_bf = _merge(o_a.reshape(m, w), o_b.reshape(m, mbw), wa_bf, wb_bf, gate, w_out[l])
    x1, h2 = _resid_norm(merged, w_out_bf, x.reshape(m, d), gt1, norm2_g[l], sc2, sh2,
                         rows_per_batch=t, norm_dtype=BF16, emit_x=True, name="out_proj_norm2")
    u, mlp_down_bf = _matmul(h2, mlp_up[l], BF16, act="relu2", side_cast=mlp_down[l], name="mlp_up")
    if out_affine is None:
        (y,) = _resid_norm(u, mlp_down_bf, x1, gt2, out_g, rows_per_batch=t, norm_dtype=F32,
                           emit_x=False, name="mlp_down_final_norm")
        return y.reshape(bsz, t, d)
    raise NotImplementedError("only a single layer followed by the final norm is implemented")


def kernel(x, c, ada_w, ada_b, norm1_g, w_in, rw_mu_rkv, rw_mu_wag, rw_w0, rw_w1, rw_w2, rw_a0, rw_a1, rw_a2, rw_g1, rw_g2, rw_k_k, rw_k_a, rw_r_k, rw_lnx_w, rw_lnx_b, mb_conv_w, mb_conv_b, mb_dt_bias, mb_a_log, mb_d, mb_norm_w, w_branch_a, w_branch_b, w_gate, b_gate, w_out, norm2_g, mlp_up, mlp_down, final_norm_g):
    depth = ada_w.shape[0]
    assert depth == 1, "the fused final-norm epilogue assumes a single layer"
    mod = _ada_mod(c, ada_w[0], ada_b[0])
    return _layer(x, mod, 0, norm1_g, w_in, rw_mu_rkv, rw_mu_wag, rw_w0, rw_w1, rw_w2, rw_a0, rw_a1, rw_a2, rw_g1,
                  rw_g2, rw_k_k, rw_k_a, rw_r_k, rw_lnx_w, rw_lnx_b, mb_conv_w, mb_conv_b, mb_dt_bias, mb_a_log,
                  mb_d, mb_norm_w, w_branch_a, w_branch_b, w_gate, b_gate, w_out, norm2_g, mlp_up, mlp_down,
                  final_norm_g, None)
```

```python
import functools
import math

import jax
import jax.numpy as jnp
from jax import lax
from jax.experimental import pallas as pl
from jax.experimental.pallas import tpu as pltpu

F32 = jnp.float32
BF16 = jnp.bfloat16

LANES = 128
HEAD = 64
RW_CHUNK = 64
RW_DECAY_LORA = 96
RW_AAA_LORA = 96
RW_GATE_LORA = 256
RW_LNX_EPS = 64e-5
MB_GROUPS = 8
MB_STATE = 128
MB_CONV = 4
MB_CHUNK = 128
NORM_EPS = 1e-5
LOG2E = 1.4426950408889634
VMEM_LIMIT = 56 * 1024 * 1024

SM_W, SM_A, SM_G, SM_DT, SM_WIDTH = 0, 128, 256, 512, 640


def _params(sem):
    return pltpu.CompilerParams(dimension_semantics=sem, vmem_limit_bytes=VMEM_LIMIT)


def _sigmoid(x):
    return 1.0 / (1.0 + jnp.exp(-x))


def _softplus(x):
    return jnp.maximum(x, 0.0) + jnp.log1p(jnp.exp(-jnp.abs(x)))


def _dot(a, b):
    return jnp.dot(a, b, preferred_element_type=F32)


def _dot_nt(a, b):
    return lax.dot_general(a, b, (((1,), (1,)), ((), ())), preferred_element_type=F32)


def _dot_tn(a, b):
    return lax.dot_general(a, b, (((0,), (0,)), ((), ())), preferred_element_type=F32)


def _split_bf16(x, terms):
    parts = []
    for _ in range(terms - 1):
        hi = x.astype(BF16)
        parts.append(hi)
        x = x - hi.astype(F32)
    parts.append(x.astype(BF16))
    return parts


def _dot_split(dot_fn, x, m, x_is_lhs, terms=3):
    parts = _split_bf16(x, terms)
    prods = [dot_fn(p, m) if x_is_lhs else dot_fn(m, p) for p in parts]
    out = prods[0]
    for p in prods[1:]:
        out = out + p
    return out


def _ada_kernel(c_ref, w_ref, b_ref, o_ref):
    c = c_ref[...]
    ca = (c * _sigmoid(c)).astype(BF16)
    o_ref[...] = _dot(ca, w_ref[...].astype(BF16)) + b_ref[...]


def _ada_mod(c, ada_w, ada_b):
    bsz, d = c.shape
    n = ada_w.shape[1]
    rows = 8
    c_pad = jnp.zeros((rows, d), F32).at[:bsz].set(c)
    tn = 1024
    out = pl.pallas_call(
        _ada_kernel,
        grid=(n // tn,),
        in_specs=[pl.BlockSpec((rows, d), lambda j: (0, 0)),
                  pl.BlockSpec((d, tn), lambda j: (0, j)),
                  pl.BlockSpec((1, tn), lambda j: (0, j))],
        out_specs=pl.BlockSpec((rows, tn), lambda j: (0, j)),
        out_shape=jax.ShapeDtypeStruct((rows, n), F32),
        compiler_params=_params(("parallel",)),
        name="ada_mod",
    )(c_pad, ada_w, ada_b.reshape(1, n))
    return out[:bsz]


def _small_proj_weights(mu_wag, w1, a1, g1, w_dt):
    lora = [_pad_cols(w1, LANES), _pad_cols(a1, LANES), g1]
    direct = [(1.0 - mu_wag[i])[:, None] * w for i, w in enumerate(lora)]
    shifted = [mu_wag[i][:, None] * w for i, w in enumerate(lora)]
    return jnp.concatenate(direct + [_pad_cols(w_dt, LANES)] + shifted, axis=1).astype(BF16)


def _norm1_kernel(x_ref, g_ref, sc_ref, sh_ref, ws_ref, h_ref, sm_ref, prev_ref):
    i = pl.program_id(1)

    @pl.when(i == 0)
    def _():
        prev_ref[...] = jnp.zeros_like(prev_ref)

    x = x_ref[...]
    tm = x.shape[0]
    ms = jnp.mean(x * x, axis=-1, keepdims=True)
    h = x * lax.rsqrt(ms + NORM_EPS) * g_ref[...] * (1.0 + sc_ref[...]) + sh_ref[...]
    hb = h.astype(BF16)
    h_ref[...] = hb
    p = _dot(hb, ws_ref[...])
    q = p[:, SM_WIDTH:]
    qs = pltpu.roll(q, 1, 0)
    row = lax.broadcasted_iota(jnp.int32, (8, SM_DT), 0)
    qs = jnp.concatenate([jnp.where(row == 0, prev_ref[0:1, :], qs[0:8]), qs[8:]], axis=0)
    prev_ref[0:1, :] = q[tm - 1:tm, :]
    sm_ref[:, :SM_DT] = p[:, :SM_DT] + qs
    sm_ref[:, SM_DT:] = p[:, SM_DT:SM_WIDTH]


def _norm1(x, g, sc, sh, w_small, tm=256):
    bsz, t, d = x.shape
    return pl.pallas_call(
        _norm1_kernel,
        grid=(bsz, t // tm),
        in_specs=[pl.BlockSpec((None, tm, d), lambda b, i: (b, i, 0)),
                  pl.BlockSpec((1, d), lambda b, i: (0, 0)),
                  pl.BlockSpec((None, 1, d), lambda b, i: (b, 0, 0)),
                  pl.BlockSpec((None, 1, d), lambda b, i: (b, 0, 0)),
                  pl.BlockSpec((d, SM_WIDTH + SM_DT), lambda b, i: (0, 0))],
        out_specs=[pl.BlockSpec((None, tm, d), lambda b, i: (b, i, 0)),
                   pl.BlockSpec((None, tm, SM_WIDTH), lambda b, i: (b, i, 0))],
        out_shape=[jax.ShapeDtypeStruct((bsz, t, d), BF16),
                   jax.ShapeDtypeStruct((bsz, t, SM_WIDTH), F32)],
        scratch_shapes=[pltpu.VMEM((8, SM_DT), F32)],
        compiler_params=_params(("parallel", "arbitrary")),
        name="norm1_small_proj",
    )(x, g.reshape(1, d), sc.reshape(bsz, 1, d), sh.reshape(bsz, 1, d), w_small)


def _mm_kernel(a_ref, w_ref, *rest, act, has_bias, has_side, w_transposed, shift_period):
    rest = list(rest)
    bias_ref = rest.pop(0) if has_bias else None
    mu_ref = rest.pop(0) if shift_period else None
    side_ref = rest.pop(0) if has_side else None
    o_ref = rest.pop(0)
    side_o_ref = rest.pop(0) if has_side else None
    wb_ref = rest.pop(0)
    carry_ref = rest.pop(0) if shift_period else None

    @pl.when(pl.program_id(1) == 0)
    def _():
        w = w_ref[...]
        wb_ref[...] = (w.T if w_transposed else w).astype(BF16)

    if has_side:
        side_o_ref[...] = side_ref[...].astype(BF16)

    def epilogue(y):
        if has_bias:
            y = y + bias_ref[...]
        if act == "sigmoid":
            y = _sigmoid(y)
        elif act == "relu2":
            y = jnp.maximum(y, 0.0)
            y = y * y
        return y

    if not shift_period:
        tm = o_ref.shape[0]
        sub = min(256, tm)
        for r0 in range(0, tm, sub):
            o_ref[r0:r0 + sub, :] = epilogue(_dot(a_ref[r0:r0 + sub, :], wb_ref[...])).astype(o_ref.dtype)
        return

    @pl.when(pl.program_id(1) % shift_period == 0)
    def _():
        carry_ref[...] = jnp.zeros_like(carry_ref)

    tm, tn = o_ref.shape
    sub = min(256, tm)
    row = lax.broadcasted_iota(jnp.int32, (8, tn), 0)
    before = carry_ref[0:1, :]
    for r0 in range(0, tm, sub):
        y = epilogue(_dot(a_ref[r0:r0 + sub, :], wb_ref[...]))
        ys = pltpu.roll(y, 1, 0)
        ys = jnp.concatenate([jnp.where(row == 0, before, ys[0:8]), ys[8:]], axis=0)
        before = y[sub - 1:sub, :]
        o_ref[r0:r0 + sub, :] = (y + (ys - y) * mu_ref[...]).astype(o_ref.dtype)
    carry_ref[0:1, :] = before


def _matmul(a, w, out_dtype, col0=0, n=None, bias=None, act=None, w_transposed=False, side_cast=None,
            shift_mu=None, rows_per_seq=None, tm=1024, tn=1024, name="matmul"):
    m, k = a.shape
    n = w.shape[0 if w_transposed else 1] if n is None else n
    tm, tn = min(tm, m), min(tn, n)
    assert col0 % tn == 0 and n % tn == 0 and m % tm == 0
    jb = col0 // tn
    w_spec = (pl.BlockSpec((tn, k), lambda j, i: (j + jb, 0)) if w_transposed
              else pl.BlockSpec((k, tn), lambda j, i: (0, j + jb)))
    in_specs = [pl.BlockSpec((tm, k), lambda j, i: (i, 0)), w_spec]
    args = [a, w]
    if bias is not None:
        in_specs.append(pl.BlockSpec((1, tn), lambda j, i: (0, j)))
        args.append(bias.reshape(1, n).astype(F32))
    scratch = [pltpu.VMEM((k, tn), BF16)]
    shift_period = 0
    if shift_mu is not None:
        assert rows_per_seq % tm == 0
        shift_period = rows_per_seq // tm
        in_specs.append(pl.BlockSpec((1, tn), lambda j, i: (0, j)))
        args.append(shift_mu.reshape(1, n).astype(F32))
        scratch.append(pltpu.VMEM((8, tn), F32))
    out_specs = [pl.BlockSpec((tm, tn), lambda j, i: (i, j))]
    out_shape = [jax.ShapeDtypeStruct((m, n), out_dtype)]
    if side_cast is not None:
        ni = m // tm
        rows = side_cast.shape[0] // ((n // tn) * ni)
        assert rows * (n // tn) * ni == side_cast.shape[0] and rows % 16 == 0
        side_spec = pl.BlockSpec((rows, side_cast.shape[1]), lambda j, i: (j * ni + i, 0))
        in_specs.append(side_spec)
        args.append(side_cast)
        out_specs.append(side_spec)
        out_shape.append(jax.ShapeDtypeStruct(side_cast.shape, BF16))
    outs = pl.pallas_call(
        functools.partial(_mm_kernel, act=act, has_bias=bias is not None, has_side=side_cast is not None,
                          w_transposed=w_transposed, shift_period=shift_period),
        grid=(n // tn, m // tm),
        in_specs=in_specs,
        out_specs=out_specs,
        out_shape=out_shape,
        scratch_shapes=scratch,
        compiler_params=_params(("parallel", "arbitrary")),
        name=name,
    )(*args)
    return outs if side_cast is not None else outs[0]


def _merge_kernel(oa_ref, ob_ref, wa_ref, wb_ref, ga_ref, gb_ref, side_ref, o_ref, side_o_ref):
    ya = _dot(oa_ref[...], wa_ref[...])
    yb = _dot(ob_ref[...], wb_ref[...])
    o_ref[...] = (ga_ref[...].astype(F32) * ya + gb_ref[...].astype(F32) * yb).astype(o_ref.dtype)
    side_o_ref[...] = side_ref[...].astype(BF16)


def _merge(o_a, o_b, wa, wb, gate, side_cast, tm=512, tn=1024):
    m, ka = o_a.shape
    kb = o_b.shape[1]
    n = wa.shape[1]
    off = n // tn
    ni = m // tm
    rows = side_cast.shape[0] // (off * ni)
    assert rows * off * ni == side_cast.shape[0] and rows % 16 == 0
    side_spec = pl.BlockSpec((rows, side_cast.shape[1]), lambda j, i: (j * ni + i, 0))
    return pl.pallas_call(
        _merge_kernel,
        grid=(n // tn, m // tm),
        in_specs=[pl.BlockSpec((tm, ka), lambda j, i: (i, 0)),
                  pl.BlockSpec((tm, kb), lambda j, i: (i, 0)),
                  pl.BlockSpec((ka, tn), lambda j, i: (0, j)),
                  pl.BlockSpec((kb, tn), lambda j, i: (0, j)),
                  pl.BlockSpec((tm, tn), lambda j, i: (i, j)),
                  pl.BlockSpec((tm, tn), lambda j, i: (i, j + off)),
                  side_spec],
        out_specs=[pl.BlockSpec((tm, tn), lambda j, i: (i, j)), side_spec],
        out_shape=[jax.ShapeDtypeStruct((m, n), BF16), jax.ShapeDtypeStruct(side_cast.shape, BF16)],
        compiler_params=_params(("parallel", "parallel")),
        name="branch_merge",
    )(o_a, o_b, wa, wb, gate, gate, side_cast)


def _resid_norm_kernel(a_ref, b_ref, x_ref, gt_ref, g_ref, *rest, affine, emit_x):
    rest = list(rest)
    if affine:
        sc_ref, sh_ref = rest[0], rest[1]
        rest = rest[2:]
    if emit_x:
        xo_ref, no_ref, acc_ref = rest
    else:
        no_ref, acc_ref = rest
    kk = pl.program_id(1)

    @pl.when(kk == 0)
    def _():
        acc_ref[...] = jnp.zeros_like(acc_ref)

    acc_ref[...] += _dot(a_ref[...], b_ref[...])

    @pl.when(kk == pl.num_programs(1) - 1)
    def _():
        xn = x_ref[...] + gt_ref[...] * acc_ref[...]
        if emit_x:
            xo_ref[...] = xn
        ms = jnp.mean(xn * xn, axis=-1, keepdims=True)
        y = xn * lax.rsqrt(ms + NORM_EPS) * g_ref[...]
        if affine:
            y = y * (1.0 + sc_ref[...]) + sh_ref[...]
        no_ref[...] = y.astype(no_ref.dtype)


def _resid_norm(a, b, x, gt, g, sc=None, sh=None, *, rows_per_batch, norm_dtype, emit_x, tm=512, tk=2048,
                name="resid_norm"):
    m, k = a.shape
    n = b.shape[1]
    bsz = gt.shape[0]
    tpb = rows_per_batch // tm
    affine = sc is not None
    vec = lambda v: v.reshape(bsz, 1, n)
    in_specs = [pl.BlockSpec((tm, tk), lambda i, kk: (i, kk)),
                pl.BlockSpec((tk, n), lambda i, kk: (kk, 0)),
                pl.BlockSpec((tm, n), lambda i, kk: (i, 0)),
                pl.BlockSpec((None, 1, n), lambda i, kk: (i // tpb, 0, 0)),
                pl.BlockSpec((1, n), lambda i, kk: (0, 0))]
    args = [a, b, x, vec(gt), g.reshape(1, n)]
    if affine:
        in_specs += [pl.BlockSpec((None, 1, n), lambda i, kk: (i // tpb, 0, 0))] * 2
        args += [vec(sc), vec(sh)]
    out_specs = [pl.BlockSpec((tm, n), lambda i, kk: (i, 0))]
    out_shape = [jax.ShapeDtypeStruct((m, n), norm_dtype)]
    if emit_x:
        out_specs = [pl.BlockSpec((tm, n), lambda i, kk: (i, 0))] + out_specs
        out_shape = [jax.ShapeDtypeStruct((m, n), F32)] + out_shape
    return pl.pallas_call(
        functools.partial(_resid_norm_kernel, affine=affine, emit_x=emit_x),
        grid=(m // tm, k // tk),
        in_specs=in_specs,
        out_specs=out_specs,
        out_shape=out_shape,
        scratch_shapes=[pltpu.VMEM((tm, n), F32)],
        compiler_params=_params(("parallel", "arbitrary")),
        name=name,
    )(*args)


def _seg_sum(x, ones_bd, terms):
    cols = x.shape[1]
    wid = ones_bd.shape[0]
    outs = [_dot_split(_dot, x[:, c:c + wid], ones_bd, True, terms=terms) for c in range(0, cols, wid)]
    return outs[0] if len(outs) == 1 else jnp.concatenate(outs, axis=1)


def _inv_unit_lower_all(a_list, ri, ci):
    n = a_list[0].shape[0]
    eye = (ri == ci).astype(F32)
    same16 = (ri // 16) == (ci // 16)
    same32 = (ri // 32) == (ci // 32)
    t = [eye + jnp.where(same16, a, 0.0) for a in a_list]
    p = [jnp.where(same16, a, 0.0).astype(BF16) for a in a_list]
    p = [_dot(x, x).astype(BF16) for x in p]
    for _ in range(2):
        r = [_dot(x, jnp.concatenate([tt.astype(BF16), x], axis=1)) for x, tt in zip(p, t)]
        t = [tt + rr[:, :n] for tt, rr in zip(t, r)]
        p = [rr[:, n:].astype(BF16) for rr in r]
    t = [tt + _dot(x, tt.astype(BF16)) for x, tt in zip(p, t)]
    for half, sel in ((16, same32 & (~same16)), (32, ~same32)):
        lo = [s for s in range(0, n, half) if (s // half) % 2 == 1]
        tb = [tt.astype(BF16) for tt in t]
        t_lo = [jnp.concatenate([tt[s:s + half] for s in lo], axis=0).astype(BF16) for tt in t]
        x = [_dot(tl, jnp.where(sel, a, 0.0).astype(BF16)).astype(BF16) for tl, a in zip(t_lo, a_list)]
        upd = [_dot(xx, b) for xx, b in zip(x, tb)]

        def add_rows(tt, u):
            parts = [tt[s:s + half] for s in range(0, n, half)]
            for j, s in enumerate(lo):
                parts[s // half] = parts[s // half] + u[j * half:(j + 1) * half]
            return jnp.concatenate(parts, axis=0)

        t = [add_rows(tt, u) for tt, u in zip(t, upd)]
    return t


def _rwkv_kernel(pr_ref, pk_ref, pv_ref, sm_ref, w0_ref, w2_ref, a0_ref, a2_ref,
                 g2_ref, kk_ref, ka_ref, rk_ref, lnw_ref, lnb_ref, o_ref, s_ref, y_ref):
    tb, hw = pr_ref.shape
    n_chunk = tb // RW_CHUNK
    n_pair = hw // LANES
    c_len = RW_CHUNK

    @pl.when(pl.program_id(2) == 0)
    def _():
        s_ref[...] = jnp.zeros_like(s_ref)

    r, k, v = pr_ref[...], pk_ref[...], pv_ref[...]

    lw = jnp.tanh(sm_ref[:, SM_W:SM_A]).astype(BF16)
    logw = (-math.exp(-0.5) * LOG2E) * _sigmoid(w0_ref[...] + _dot(lw, w2_ref[...]))
    a = _sigmoid(a0_ref[...] + _dot(sm_ref[:, SM_A:SM_G].astype(BF16), a2_ref[...]))
    g = _dot(_sigmoid(sm_ref[:, SM_G:SM_DT]).astype(BF16), g2_ref[...])

    li = lax.broadcasted_iota(jnp.int32, (LANES, LANES), 0)
    lj = lax.broadcasted_iota(jnp.int32, (LANES, LANES), 1)
    same_head = (li // HEAD) == (lj // HEAD)
    seg_w = min(2 * LANES, hw)
    si = lax.broadcasted_iota(jnp.int32, (seg_w, seg_w), 0)
    sj = lax.broadcasted_iota(jnp.int32, (seg_w, seg_w), 1)
    ones_bd = ((si // HEAD) == (sj // HEAD)).astype(BF16)

    kkv = k * kk_ref[...]
    kkn = kkv * lax.rsqrt(jnp.maximum(_seg_sum(kkv * kkv, ones_bd, 2), 1e-24))
    k2 = k * (1.0 + (a - 1.0) * ka_ref[...])
    aa = -kkn
    bb = kkn * a
    bonus = _seg_sum(r * k2 * rk_ref[...], ones_bd, 1) * v

    ti = lax.broadcasted_iota(jnp.int32, (tb, tb), 0)
    tj = lax.broadcasted_iota(jnp.int32, (tb, tb), 1)
    cum_mat = (((ti // c_len) == (tj // c_len)) & (tj <= ti)).astype(BF16)
    lc = _dot_split(_dot, logw, cum_mat, False, terms=2)
    e_pos = jnp.exp2(lc)
    e_neg = jnp.exp2(-lc)
    en_b = e_neg.astype(BF16)
    rt = r.astype(BF16) * e_pos.astype(BF16)
    at = aa.astype(BF16) * jnp.exp2(lc - logw).astype(BF16)
    bt = bb.astype(BF16) * en_b
    kt = k2.astype(BF16) * en_b

    lane = lax.broadcasted_iota(jnp.int32, (c_len, LANES), 1)
    m0 = lane < HEAD

    def bd(x):
        xb = x.astype(BF16)
        zero = jnp.zeros_like(xb)
        return jnp.concatenate([jnp.where(m0, xb, zero), jnp.where(m0, zero, xb)], axis=0)

    strict = same_head & ((li % HEAD) > (lj % HEAD))
    incl = same_head & ((li % HEAD) >= (lj % HEAD))

    inst = [(c, p) for c in range(n_chunk) for p in range(n_pair)]
    rows = lambda c: slice(c * c_len, (c + 1) * c_len)
    lanes = lambda p: slice(p * LANES, (p + 1) * LANES)
    tile = lambda x: [bd(x[rows(c), lanes(p)]) for c, p in inst]
    a_bd, r_bd, b_bd, k_bd, v_bd = tile(at), tile(rt), tile(bt), tile(kt), tile(v)

    bk_bd = [jnp.concatenate([z, w], axis=0) for z, w in zip(b_bd, k_bd)]
    gm = [_dot_nt(jnp.concatenate([x, y], axis=0), zw) for x, y, zw in zip(a_bd, r_bd, bk_bd)]
    a_ab = [jnp.where(strict, x[0:LANES, 0:LANES], 0.0) for x in gm]
    a_ak = [jnp.where(strict, x[0:LANES, LANES:], 0.0).astype(BF16) for x in gm]
    incl2 = jnp.concatenate([incl, incl], axis=1)
    a_rbk = [jnp.where(incl2, x[LANES:, :], 0.0).astype(BF16) for x in gm]
    t_inv = [x.astype(BF16) for x in _inv_unit_lower_all(a_ab, li, lj)]
    av = [_dot(x, y).astype(BF16) for x, y in zip(a_ak, v_bd)]
    tav = [_dot(t, jnp.concatenate([x, y], axis=1)) for t, x, y in zip(t_inv, a_bd, av)]
    ta = [x[:, :LANES].astype(BF16) for x in tav]
    uv = [x[:, LANES:] for x in tav]

    state = [s_ref[p] for p in range(n_pair)]
    for c in range(n_chunk):
        ids = [c * n_pair + p for p in range(n_pair)]
        s_b = [s.astype(BF16) for s in state]
        u_bd = [(_dot_nt(ta[i], s_b[p]) + uv[i]).astype(BF16) for p, i in enumerate(ids)]
        g_end = [e_pos[(c + 1) * c_len - 1:(c + 1) * c_len, lanes(p)] for p in range(n_pair)]
        uv_bd = [jnp.concatenate([u_bd[p], v_bd[i]], axis=0) for p, i in enumerate(ids)]
        state = [g_end[p] * (state[p] + _dot_tn(uv_bd[p], bk_bd[i])) for p, i in enumerate(ids)]
        for p, i in enumerate(ids):
            y_bd = _dot_nt(r_bd[i], s_b[p]) + _dot(a_rbk[i], uv_bd[p])
            y_ref[rows(c), lanes(p)] = y_bd[0:c_len] + y_bd[c_len:]
    for p in range(n_pair):
        s_ref[p] = state[p]

    y = y_ref[...]
    mu = _seg_sum(y, ones_bd, 1) * (1.0 / HEAD)
    d = y - mu
    var = _seg_sum(d * d, ones_bd, 1) * (1.0 / HEAD)
    yn = d * lax.rsqrt(var + RW_LNX_EPS) * lnw_ref[...] + lnb_ref[...]
    o_ref[...] = ((yn + bonus) * g).astype(o_ref.dtype)


def _rwkv_branch(p_rkv, small, w0, w2, a0, a2, g2, k_k, k_a, r_k, lnx_w, lnx_b, tb=256, hw=1024):
    bsz, t, w3 = p_rkv.shape
    w = w3 // 3
    nb = w // hw
    row = lambda v: v.reshape(1, -1).astype(F32)
    vec_spec = lambda off: pl.BlockSpec((1, hw), lambda b, gi, ti: (0, gi + off))
    rkv_spec = lambda off: pl.BlockSpec((None, tb, hw), lambda b, gi, ti: (b, ti, gi + off))
    mat_spec = lambda rows: pl.BlockSpec((rows, hw), lambda b, gi, ti: (0, gi))
    return pl.pallas_call(
        _rwkv_kernel,
        grid=(bsz, nb, t // tb),
        in_specs=[rkv_spec(0), rkv_spec(nb), rkv_spec(2 * nb),
                  pl.BlockSpec((None, tb, SM_WIDTH), lambda b, gi, ti: (b, ti, 0)),
                  vec_spec(0), mat_spec(LANES), vec_spec(0), mat_spec(LANES), mat_spec(RW_GATE_LORA),
                  vec_spec(0), vec_spec(0), vec_spec(0), vec_spec(0), vec_spec(0)],
        out_specs=pl.BlockSpec((None, tb, hw), lambda b, gi, ti: (b, ti, gi)),
        out_shape=jax.ShapeDtypeStruct((bsz, t, w), BF16),
        scratch_shapes=[pltpu.VMEM((hw // LANES, LANES, LANES), F32),
                        pltpu.VMEM((tb, hw), F32)],
        compiler_params=_params(("parallel", "parallel", "arbitrary")),
        name="rwkv7_branch",
    )(p_rkv, p_rkv, p_rkv, small, row(w0), w2, row(a0), a2, g2,
      row(k_k), row(k_a), row(r_k), row(lnx_w), row(lnx_b))


def _ssd_kernel(xbc_ref, z_ref, dt_ref, cw_ref, cb_ref, dtb_ref, alog_ref, dsk_ref, nw_ref, exp_ref,
                o_ref, ext_ref, act_ref, st_ref):
    ch, cw = xbc_ref.shape
    width = z_ref.shape[1]
    gw = width // MB_GROUPS
    tail = 8

    @pl.when(pl.program_id(1) == 0)
    def _():
        st_ref[...] = jnp.zeros_like(st_ref)
        ext_ref[0:tail, :] = jnp.zeros((tail, cw), F32)

    @pl.when(pl.program_id(1) != 0)
    def _():
        ext_ref[0:tail, :] = ext_ref[ch:ch + tail, :]

    ext_ref[tail:tail + ch, :] = xbc_ref[...]

    cblk = 512
    for c0 in range(0, cw, cblk):
        e = ext_ref[:, c0:c0 + cblk]
        acc = cb_ref[:, c0:c0 + cblk] + cw_ref[MB_CONV - 1:MB_CONV, c0:c0 + cblk] * e[tail:]
        for s in range(1, MB_CONV):
            acc = acc + cw_ref[MB_CONV - 1 - s:MB_CONV - s, c0:c0 + cblk] * pltpu.roll(e, s, 0)[tail:]
        act_ref[:, c0:c0 + cblk] = acc * _sigmoid(acc)

    dt = _softplus(dt_ref[...] + dtb_ref[...])
    d_a = dt * (-jnp.exp(alog_ref[...])) * LOG2E
    ti = lax.broadcasted_iota(jnp.int32, (ch, ch), 0)
    tj = lax.broadcasted_iota(jnp.int32, (ch, ch), 1)
    causal = tj <= ti
    acs = _dot_split(_dot, d_a, causal.astype(BF16), False)
    acs_t = _dot_split(_dot_tn, d_a, (ti <= tj).astype(BF16), True)
    expand = exp_ref[...]
    dt_x = _dot_split(_dot, dt, expand, True)
    acs_x = _dot_split(_dot, acs, expand, True)
    acs_last_x = acs_x[ch - 1:ch, :]
    lane = lax.broadcasted_iota(jnp.int32, (ch, LANES), 1)
    m0 = lane < HEAD

    for gi in range(MB_GROUPS):
        gs = slice(gi * gw, (gi + 1) * gw)
        c_g = act_ref[:, width + MB_GROUPS * MB_STATE + gi * MB_STATE:width + MB_GROUPS * MB_STATE + (gi + 1) * MB_STATE].astype(BF16)
        b_g = act_ref[:, width + gi * MB_STATE:width + (gi + 1) * MB_STATE].astype(BF16)
        xs_g = act_ref[:, gs]
        acs_g = acs_x[:, gs]
        x_g = (xs_g * dt_x[:, gs]).astype(BF16)
        cb = _dot_nt(c_g, b_g).astype(BF16)
        s_old = st_ref[gi]
        y_off = _dot(c_g, s_old.astype(BF16)) * jnp.exp2(acs_g)
        y_diag = []
        for q in range(gw // LANES):
            x_pair = x_g[:, q * LANES:(q + 1) * LANES]
            x_zero = jnp.zeros_like(x_pair)
            y_pair = None
            for hh in range(2):
                head = (gi * gw) // HEAD + 2 * q + hh
                col = jnp.broadcast_to(acs[:, head:head + 1], (ch, ch))
                rowv = acs_t[head:head + 1, :]
                l_dec = jnp.exp2(jnp.where(causal, col - rowv, -jnp.inf))
                scores = cb * l_dec.astype(BF16)
                x_m = jnp.where(m0, x_pair, x_zero) if hh == 0 else jnp.where(m0, x_zero, x_pair)
                part = _dot(scores, x_m)
                y_pair = part if y_pair is None else y_pair + part
            y_diag.append(y_pair)
        y = jnp.concatenate(y_diag, axis=1) + y_off + xs_g * dsk_ref[:, gs]
        last_g = acs_last_x[:, gs]
        x_dec = x_g * jnp.exp2(last_g - acs_g).astype(BF16)
        st_ref[gi] = s_old * jnp.exp2(last_g) + _dot_tn(b_g, x_dec)
        z = z_ref[:, gs]
        yg = y * (z * _sigmoid(z))
        ms = jnp.mean(yg * yg, axis=-1, keepdims=True)
        o_ref[:, gs] = (yg * lax.rsqrt(ms + NORM_EPS) * nw_ref[:, gs]).astype(o_ref.dtype)


def _ssd_branch(p_z, p_xbc, small, conv_w, conv_b, dt_bias, a_log, d_skip, norm_w):
    bsz, t, width = p_z.shape
    cw = p_xbc.shape[2]
    heads = width // HEAD
    ch = MB_CHUNK
    pad = lambda v: jnp.zeros((1, LANES), F32).at[0, :heads].set(v.astype(F32))
    hid = jnp.arange(LANES)[:, None]
    col = jnp.arange(width)[None, :]
    expand = (hid == col // HEAD).astype(BF16)
    d_x = jnp.repeat(d_skip.astype(F32), HEAD).reshape(1, width)
    full = lambda shape: pl.BlockSpec(shape, lambda b, c: (0, 0))
    return pl.pallas_call(
        _ssd_kernel,
        grid=(bsz, t // ch),
        in_specs=[pl.BlockSpec((None, ch, cw), lambda b, c: (b, c, 0)),
                  pl.BlockSpec((None, ch, width), lambda b, c: (b, c, 0)),
                  pl.BlockSpec((None, ch, LANES), lambda b, c: (b, c, SM_DT // LANES)),
                  full((MB_CONV, cw)), full((1, cw)), full((1, LANES)), full((1, LANES)),
                  full((1, width)), full((1, width)), full((LANES, width))],
        out_specs=pl.BlockSpec((None, ch, width), lambda b, c: (b, c, 0)),
        out_shape=jax.ShapeDtypeStruct((bsz, t, width), BF16),
        scratch_shapes=[pltpu.VMEM((ch + 8, cw), F32),
                        pltpu.VMEM((ch, cw), F32),
                        pltpu.VMEM((MB_GROUPS, MB_STATE, width // MB_GROUPS), F32)],
        compiler_params=_params(("parallel", "arbitrary")),
        name="mamba2_branch",
    )(p_xbc, p_z, small, conv_w.astype(F32), conv_b.reshape(1, cw).astype(F32), pad(dt_bias), pad(a_log),
      d_x, norm_w.reshape(1, width).astype(F32), expand)


def _pad_cols(w, n):
    return jnp.zeros((w.shape[0], n), w.dtype).at[:, :w.shape[1]].set(w)


def _pad_rows(w, n):
    return jnp.zeros((n, w.shape[1]), w.dtype).at[:w.shape[0]].set(w)


def _layer(x, c_act_mod, l, norm1_g, w_in, rw_mu_rkv, rw_mu_wag, rw_w0, rw_w1, rw_w2, rw_a0, rw_a1, rw_a2, rw_g1,
           rw_g2, rw_k_k, rw_k_a, rw_r_k, rw_lnx_w, rw_lnx_b, mb_conv_w, mb_conv_b, mb_dt_bias, mb_a_log, mb_d,
           mb_norm_w, w_branch_a, w_branch_b, w_gate, b_gate, w_out, norm2_g, mlp_up, mlp_down, out_g, out_affine):
    bsz, t, d = x.shape
    m = bsz * t
    w = rw_k_k.shape[1]
    mbw = mb_norm_w.shape[1]
    cw = mb_conv_b.shape[1]
    off_rkv, off_z, off_xbc = 3 * w, 3 * w + mbw, 3 * w + mbw + cw
    sh1, sc1, gt1, sh2, sc2, gt2 = jnp.split(c_act_mod, 6, axis=-1)

    w_small = _small_proj_weights(rw_mu_wag[l], rw_w1[l], rw_a1[l], rw_g1[l], w_in[l][:, off_xbc:])
    h, small = _norm1(x, norm1_g[l], sc1, sh1, w_small)
    h2d = h.reshape(m, d)
    w_in_t = jnp.swapaxes(w_in[l], 0, 1)
    p_rkv = _matmul(h2d, w_in_t, F32, 0, off_rkv, w_transposed=True, shift_mu=rw_mu_rkv[l], rows_per_seq=t,
                    name="proj_rkv").reshape(bsz, t, off_rkv)
    p_z, wb_bf = _matmul(h2d, w_in_t, F32, off_rkv, mbw, w_transposed=True, side_cast=w_branch_b[l], name="proj_z")
    p_z = p_z.reshape(bsz, t, mbw)
    p_xbc = _matmul(h2d, w_in_t, F32, off_z, cw, w_transposed=True, name="proj_xbc").reshape(bsz, t, cw)
    gate, wa_bf = _matmul(h2d, w_gate[l], BF16, bias=b_gate[l], act="sigmoid", side_cast=w_branch_a[l],
                          name="proj_gate")

    o_a = _rwkv_branch(p_rkv, small, rw_w0[l], _pad_rows(rw_w2[l], LANES).astype(BF16), rw_a0[l],
                       _pad_rows(rw_a2[l], LANES).astype(BF16), rw_g2[l].astype(BF16), rw_k_k[l], rw_k_a[l],
                       rw_r_k[l], rw_lnx_w[l], rw_lnx_b[l])
    o_b = _ssd_branch(p_z, p_xbc, small, mb_conv_w[l], mb_conv_b[l], mb_dt_bias[l], mb_a_log[l], mb_d[l],
                      mb_norm_w[l])
    merged, w_out_bf = _merge(o_a.reshape(m, w), o_b.reshape(m, mbw), wa_bf, wb_bf, gate, w_out[l])
    x1, h2 = _resid_norm(merged, w_out_bf, x.reshape(m, d), gt1, norm2_g[l], sc2, sh2,
                         rows_per_batch=t, norm_dtype=BF16, emit_x=True, name="out_proj_norm2")
    u, mlp_down_bf = _matmul(h2, mlp_up[l], BF16, act="relu2", side_cast=mlp_down[l], name="mlp_up")
    if out_affine is None:
        (y,) = _resid_norm(u, mlp_down_bf, x1, gt2, out_g, rows_per_batch=t, norm_dtype=F32,
                           emit_x=False, name="mlp_down_final_norm")
        return y.reshape(bsz, t, d)
    raise NotImplementedError("only a single layer followed by the final norm is implemented")


def kernel(x, c, ada_w, ada_b, norm1_g, w_in, rw_mu_rkv, rw_mu_wag, rw_w0, rw_w1, rw_w2, rw_a0, rw_a1, rw_a2, rw_g1, rw_g2, rw_k_k, rw_k_a, rw_r_k, rw_lnx_w, rw_lnx_b, mb_conv_w, mb_conv_b, mb_dt_bias, mb_a_log, mb_d, mb_norm_w, w_branch_a, w_branch_b, w_gate, b_gate, w_out, norm2_g, mlp_up, mlp_down, final_norm_g):
    depth = ada_w.shape[0]
    assert depth == 1, "the fused final-norm epilogue assumes a single layer"
    mod = _ada_mod(c, ada_w[0], ada_b[0])
    return _layer(x, mod, 0, norm1_g, w_in, rw_mu_rkv, rw_mu_wag, rw_w0, rw_w1, rw_w2, rw_a0, rw_a1, rw_a2, rw_g1,
                  rw_g2, rw_k_k, rw_k_a, rw_r_k, rw_lnx_w, rw_lnx_b, mb_conv_w, mb_conv_b, mb_dt_bias, mb_a_log,
                  mb_d, mb_norm_w, w_branch_a, w_branch_b, w_gate, b_gate, w_out, norm2_g, mlp_up, mlp_down,
                  final_norm_g, None)
```
